```python
import math, functools
import jax, jax.numpy as jnp
from jax import lax
import numpy as np

D_MODEL = 1024
BATCH = 8
SEQ = 2048
DEPTH = 2
DEC_BATCH = 32
DEC_SEQ = 8
PAST_LEN = 16384
PAGE_SIZE = 128

N_MEM = 256
GDN_HEADS = 4
GDN_DK = 128
GDN_DV = 128
GDN_CONV = 4
GDN_CHUNK = 64
GDN_QK_W = GDN_HEADS * GDN_DK
GDN_V_W = GDN_HEADS * GDN_DV
GDN_QKV_W = 2 * GDN_QK_W + GDN_V_W
DIFF_HEADS = 4
DIFF_DQK = 64
DIFF_DV = 2 * DIFF_DQK
DIFF_QK_W = DIFF_HEADS * 2 * DIFF_DQK
DIFF_V_W = DIFF_HEADS * DIFF_DV
DIFF_QBLOCK = 128
XA_HEADS = 4
XA_DH = 128
XA_W = XA_HEADS * XA_DH
N_BRANCH = 3
BRANCH_W = 512
D_FF = 4 * D_MODEL
RMS_EPS = 1e-6
L2_EPS = 1e-6
IN_WIDTHS = (GDN_QKV_W, GDN_V_W, GDN_HEADS, GDN_HEADS, DIFF_QK_W, DIFF_QK_W, DIFF_V_W, XA_W, N_BRANCH * D_MODEL)
IN_COLS = GDN_QKV_W + GDN_V_W + 2 * GDN_HEADS + 2 * DIFF_QK_W + DIFF_V_W + XA_W + N_BRANCH * D_MODEL

kernel_name = 'hybrid_gdn_diffattn_memxattn_decode_step'

F32 = jnp.float32


def rmsnorm(x, w):
    xf = x.astype(F32)
    y = xf * lax.rsqrt(jnp.mean(xf * xf, axis=-1, keepdims=True) + RMS_EPS)
    return (y * w.astype(F32)).astype(x.dtype)


def l2norm(x):
    xf = x.astype(F32)
    return xf * lax.rsqrt(jnp.sum(xf * xf, axis=-1, keepdims=True) + L2_EPS)


def split_cols(y, widths):
    cuts = [int(c) for c in np.cumsum(widths)[:-1]]
    return jnp.split(y, cuts, axis=-1)


def short_conv(u, buf, w):
    L = u.shape[1]
    up = jnp.concatenate([buf.astype(u.dtype), u], axis=1)
    y = sum(up[:, i:i + L] * w[i] for i in range(GDN_CONV))
    return jax.nn.silu(y), up[:, -(GDN_CONV - 1):]


def gated_delta_rule(q, k, v, g, beta, s0):
    B, L, H, DK = q.shape
    DV = v.shape[-1]
    C = min(GDN_CHUNK, L)
    n = -(-L // C)
    pad = n * C - L

    def to_chunks(t):
        t = jnp.pad(t.astype(F32), [(0, 0), (0, pad)] + [(0, 0)] * (t.ndim - 2))
        t = t.reshape((B, n, C) + t.shape[2:])
        return jnp.moveaxis(jnp.moveaxis(t, 1, 0), 2, 3)

    qc = to_chunks(q) * (DK ** -0.5)
    kc, vc, gc, bc = to_chunks(k), to_chunks(v), to_chunks(g), to_chunks(beta)
    G = jnp.cumsum(gc, axis=-1)
    idx = jnp.arange(C)
    incl = idx[:, None] >= idx[None, :]
    strict = idx[:, None] > idx[None, :]
    gdiff = G[..., :, None] - G[..., None, :]
    decay = jnp.where(incl, jnp.exp(jnp.where(incl, gdiff, 0.0)), 0.0)
    kb = kc * bc[..., None]
    lmat = jnp.where(strict, jnp.einsum('nbhid,nbhjd->nbhij', kb, kc) * decay, 0.0)
    a_mat = jnp.eye(C, dtype=F32) + lmat
    rhs = jnp.concatenate([vc * bc[..., None], kb * jnp.exp(G)[..., None]], axis=-1)
    sol = lax.linalg.triangular_solve(a_mat, rhs, left_side=True, lower=True)
    u0, wcd = sol[..., :DV], sol[..., DV:]
    qk = jnp.einsum('nbhid,nbhjd->nbhij', qc, kc) * decay
    qg = qc * jnp.exp(G)[..., None]
    kg = kc * jnp.exp(G[..., -1:] - G)[..., None]
    g_last = jnp.exp(G[..., -1])

    def step(s, xs):
        u0_c, w_c, qk_c, qg_c, kg_c, gl_c = xs
        u = u0_c - jnp.einsum('bhcd,bhde->bhce', w_c, s)
        o = jnp.einsum('bhcd,bhde->bhce', qg_c, s) + jnp.einsum('bhij,bhje->bhie', qk_c, u)
        s = s * gl_c[..., None, None] + jnp.einsum('bhcd,bhce->bhde', kg_c, u)
        return s, o

    s_final, o = lax.scan(step, s0.astype(F32), (u0, wcd, qk, qg, kg, g_last))
    o = jnp.transpose(o, (1, 0, 3, 2, 4)).reshape(B, n * C, H, DV)[:, :L]
    return o, s_final


def diff_attend_prompt(q, k, v, lam):
    B, S, H = q.shape[:3]
    nb = S // DIFF_QBLOCK
    qb = jnp.moveaxis(q.reshape(B, nb, DIFF_QBLOCK, H, 2, DIFF_DQK), 1, 0)
    kf, vf = k.astype(F32), v.astype(F32)
    kpos = jnp.arange(S)
    scale = DIFF_DQK ** -0.5

    def block(args):
        qblk, i = args
        s = jnp.einsum('bqhcd,bkhcd->bhcqk', qblk.astype(F32), kf) * scale
        qpos = i * DIFF_QBLOCK + jnp.arange(DIFF_QBLOCK)
        s = jnp.where(kpos[None, :] <= qpos[:, None], s, -jnp.inf)
        p = jax.nn.softmax(s, axis=-1)
        a = p[:, :, 0] - lam * p[:, :, 1]
        return jnp.einsum('bhqk,bkhe->bqhe', a, vf)

    out = lax.map(block, (qb, jnp.arange(nb)))
    return jnp.moveaxis(out, 0, 1).reshape(B, S, H, DIFF_DV)


def diff_attend_paged(q, k, v, lam, cache_k, cache_v, page_table, layer):
    Bd, T, H = q.shape[:3]
    qf = q.astype(F32) * (DIFF_DQK ** -0.5)
    s = jnp.einsum('bqhcd,bkhcd->bhcqk', qf, k.astype(F32))
    causal = jnp.arange(T)[:, None] >= jnp.arange(T)[None, :]
    s = jnp.where(causal, s, -jnp.inf)
    m = jnp.max(s, axis=-1)
    p = jnp.exp(s - m[..., None])
    den = jnp.sum(p, axis=-1)
    acc = jnp.einsum('bhcqk,bkhe->bhcqe', p, v.astype(F32))

    def step(carry, phys):
        m, den, acc = carry
        kp = cache_k[layer, phys].astype(F32).reshape(Bd, PAGE_SIZE, H, 2, DIFF_DQK)
        vp = cache_v[layer, phys].astype(F32)
        s = jnp.einsum('bqhcd,bkhcd->bhcqk', qf, kp)
        m_new = jnp.maximum(m, jnp.max(s, axis=-1))
        alpha = jnp.exp(m - m_new)
        p = jnp.exp(s - m_new[..., None])
        den = den * alpha + jnp.sum(p, axis=-1)
        acc = acc * alpha[..., None] + jnp.einsum('bhcqk,bkhe->bhcqe', p, vp)
        return (m_new, den, acc), None

    (m, den, acc), _ = lax.scan(step, (m, den, acc), page_table.T)
    o = acc / den[..., None]
    a = o[:, :, 0] - lam * o[:, :, 1]
    return jnp.transpose(a, (0, 2, 1, 3))


def memory_kv(mem, g_mem, w_mk, w_mv):
    B, N, _ = mem.shape
    m = rmsnorm(mem, g_mem)
    mk = (m @ w_mk).reshape(B, N, XA_HEADS, XA_DH)
    mv = (m @ w_mv).reshape(B, N, XA_HEADS, XA_DH)
    return mk, mv


def cross_attend(q, mk, mv):
    s = jnp.einsum('bqhd,bkhd->bhqk', q.astype(F32), mk.astype(F32)) * (XA_DH ** -0.5)
    p = jax.nn.softmax(s, axis=-1)
    return jnp.einsum('bhqk,bkhd->bqhd', p, mv.astype(F32))


def trunk_layer(x, p, lam_init, conv_buf, gdn_s0, mem_k, mem_v, attend):
    B, L, _ = x.shape
    dt = x.dtype
    h = rmsnorm(x, p['g_pre_mix'])
    qkv, z, a, b, dq, dk, dv, xq, gate_logits = split_cols(h @ p['w_in'], IN_WIDTHS)
    qkv_c, new_buf = short_conv(qkv, conv_buf, p['w_conv'])
    gq, gk, gv = split_cols(qkv_c, (GDN_QK_W, GDN_QK_W, GDN_V_W))
    gq = l2norm(gq.reshape(B, L, GDN_HEADS, GDN_DK))
    gk = l2norm(gk.reshape(B, L, GDN_HEADS, GDN_DK))
    gv = gv.reshape(B, L, GDN_HEADS, GDN_DV).astype(F32)
    beta = jax.nn.sigmoid(b.astype(F32))
    log_decay = -jnp.exp(p['gdn_a_log'].astype(F32)) * jax.nn.softplus(a.astype(F32) + p['gdn_dt_bias'].astype(F32))
    o_a, s_new = gated_delta_rule(gq, gk, gv, log_decay, beta, gdn_s0)
    o_a = rmsnorm(o_a, p['g_gdn_out']) * jax.nn.silu(z.reshape(B, L, GDN_HEADS, GDN_DV).astype(F32))
    lam_p = p['diff_lambda'].astype(F32)
    lam = jnp.exp(jnp.sum(lam_p[0] * lam_p[1])) - jnp.exp(jnp.sum(lam_p[2] * lam_p[3])) + lam_init
    k_rows = dk.reshape(B, L, DIFF_HEADS, 2 * DIFF_DQK)
    v_rows = dv.reshape(B, L, DIFF_HEADS, DIFF_DV)
    o_b = attend(dq.reshape(B, L, DIFF_HEADS, 2, DIFF_DQK), k_rows.reshape(B, L, DIFF_HEADS, 2, DIFF_DQK), v_rows, lam)
    o_b = rmsnorm(o_b, p['g_diff_sub']) * (1.0 - lam_init)
    o_c = cross_attend(xq.reshape(B, L, XA_HEADS, XA_DH), mem_k, mem_v)
    branches = jnp.stack([o_a.reshape(B, L, BRANCH_W), o_b.reshape(B, L, BRANCH_W), o_c.reshape(B, L, BRANCH_W)], axis=2).astype(dt)
    up = jnp.einsum('blnw,nwd->blnd', branches, p['w_branch'])
    gates = jax.nn.sigmoid(gate_logits.astype(F32)).reshape(B, L, N_BRANCH, D_MODEL)
    merged = jnp.sum(gates * up.astype(F32), axis=2).astype(dt)
    x = x + rmsnorm(merged @ p['w_out'], p['g_post_mix'])
    h2 = rmsnorm(x, p['g_pre_mlp'])
    f = jnp.square(jax.nn.relu(h2 @ p['w_ff1'])) @ p['w_ff2']
    x = x + rmsnorm(f, p['g_post_mlp'])
    return x, k_rows, v_rows, s_new, new_buf


def setup_inputs(seed: int = 0) -> dict:
    key = jax.random.key(seed)
    ks = iter(jax.random.split(key, 40))
    n_pages = PAST_LEN // PAGE_SIZE
    n_used = DEC_BATCH * n_pages
    n_pool = n_used + n_used // 4

    def nrm(shape, scale=1.0):
        return jax.random.normal(next(ks), shape, jnp.float32) * scale

    def gain(shape):
        return 1.0 + nrm(shape, 0.02)

    page_table = jax.random.permutation(next(ks), n_pool)[:n_used].reshape(DEC_BATCH, n_pages).astype(jnp.int32)
    dt0 = jnp.exp(jax.random.uniform(next(ks), (DEPTH, GDN_HEADS), jnp.float32, math.log(1e-3), math.log(1e-1)))
    dt_bias = dt0 + jnp.log(-jnp.expm1(-dt0))
    a_log = jnp.log(jax.random.uniform(next(ks), (DEPTH, GDN_HEADS), jnp.float32, 1.0, 16.0))
    return {
        'x_prompt': nrm((BATCH, SEQ, D_MODEL)),
        'x_sample': nrm((DEC_BATCH, DEC_SEQ, D_MODEL)),
        'mem_prompt': nrm((BATCH, N_MEM, D_MODEL)),
        'cache_diff_k': nrm((DEPTH, n_pool, PAGE_SIZE, DIFF_HEADS, 2 * DIFF_DQK)),
        'cache_diff_v': nrm((DEPTH, n_pool, PAGE_SIZE, DIFF_HEADS, DIFF_DV)),
        'page_table': page_table,
        'state_gdn': nrm((DEPTH, DEC_BATCH, GDN_HEADS, GDN_DK, GDN_DV), GDN_DK ** -0.5),
        'cache_gdn_conv': nrm((DEPTH, DEC_BATCH, GDN_CONV - 1, GDN_QKV_W)),
        'cache_mem_k': nrm((DEPTH, DEC_BATCH, N_MEM, XA_HEADS, XA_DH)),
        'cache_mem_v': nrm((DEPTH, DEC_BATCH, N_MEM, XA_HEADS, XA_DH)),
        'g_pre_mix': gain((DEPTH, D_MODEL)),
        'w_in': nrm((DEPTH, D_MODEL, IN_COLS), D_MODEL ** -0.5),
        'w_conv': nrm((DEPTH, GDN_CONV, GDN_QKV_W), GDN_CONV ** -0.5),
        'gdn_a_log': a_log,
        'gdn_dt_bias': dt_bias,
        'g_gdn_out': gain((DEPTH, GDN_DV)),
        'diff_lambda': nrm((DEPTH, 4, DIFF_DQK), 0.1),
        'g_diff_sub': gain((DEPTH, DIFF_DV)),
        'g_mem': gain((DEPTH, D_MODEL)),
        'w_mem_k': nrm((DEPTH, D_MODEL, XA_W), D_MODEL ** -0.5),
        'w_mem_v': nrm((DEPTH, D_MODEL, XA_W), D_MODEL ** -0.5),
        'w_branch': nrm((DEPTH, N_BRANCH, BRANCH_W, D_MODEL), BRANCH_W ** -0.5),
        'w_out': nrm((DEPTH, D_MODEL, D_MODEL), D_MODEL ** -0.5),
        'g_post_mix': gain((DEPTH, D_MODEL)),
        'g_pre_mlp': gain((DEPTH, D_MODEL)),
        'w_ff1': nrm((DEPTH, D_MODEL, D_FF), D_MODEL ** -0.5),
        'w_ff2': nrm((DEPTH, D_FF, D_MODEL), D_FF ** -0.5),
        'g_post_mlp': gain((DEPTH, D_MODEL)),
    }


def reference(x_prompt, x_sample, mem_prompt, cache_diff_k, cache_diff_v, page_table, state_gdn, cache_gdn_conv,
              cache_mem_k, cache_mem_v, g_pre_mix, w_in, w_conv, gdn_a_log, gdn_dt_bias, g_gdn_out, diff_lambda,
              g_diff_sub, g_mem, w_mem_k, w_mem_v, w_branch, w_out, g_post_mix, g_pre_mlp, w_ff1, w_ff2, g_post_mlp):
    xp, xs = x_prompt, x_sample
    bp = xp.shape[0]
    kp_l, vp_l, sp_l, cp_l, mk_l, mv_l = [], [], [], [], [], []
    ks_l, vs_l, ss_l, cs_l = [], [], [], []
    for l in range(DEPTH):
        p = {
            'g_pre_mix': g_pre_mix[l], 'w_in': w_in[l], 'w_conv': w_conv[l], 'gdn_a_log': gdn_a_log[l],
            'gdn_dt_bias': gdn_dt_bias[l], 'g_gdn_out': g_gdn_out[l], 'diff_lambda': diff_lambda[l],
            'g_diff_sub': g_diff_sub[l], 'w_branch': w_branch[l], 'w_out': w_out[l], 'g_post_mix': g_post_mix[l],
            'g_pre_mlp': g_pre_mlp[l], 'w_ff1': w_ff1[l], 'w_ff2': w_ff2[l], 'g_post_mlp': g_post_mlp[l],
        }
        lam_init = 0.8 - 0.6 * math.exp(-0.3 * l)
        mk, mv = memory_kv(mem_prompt, g_mem[l], w_mem_k[l], w_mem_v[l])
        xp, kp, vp, sp, cp = trunk_layer(
            xp, p, lam_init,
            jnp.zeros((bp, GDN_CONV - 1, GDN_QKV_W), xp.dtype),
            jnp.zeros((bp, GDN_HEADS, GDN_DK, GDN_DV), F32),
            mk, mv, diff_attend_prompt)
        kp_l.append(kp); vp_l.append(vp); sp_l.append(sp); cp_l.append(cp); mk_l.append(mk); mv_l.append(mv)
        sample_attend = functools.partial(diff_attend_paged, cache_k=cache_diff_k, cache_v=cache_diff_v,
                                          page_table=page_table, layer=l)
        xs, k_s, v_s, s_s, c_s = trunk_layer(
            xs, p, lam_init, cache_gdn_conv[l], state_gdn[l], cache_mem_k[l], cache_mem_v[l], sample_attend)
        ks_l.append(k_s); vs_l.append(v_s); ss_l.append(s_s); cs_l.append(c_s)
    return (xp, xs,
            jnp.stack(kp_l), jnp.stack(vp_l), jnp.stack(sp_l), jnp.stack(cp_l), jnp.stack(mk_l), jnp.stack(mv_l),
            jnp.stack(ks_l), jnp.stack(vs_l), jnp.stack(ss_l), jnp.stack(cs_l))
```

```python
import functools
import math

import jax
import jax.numpy as jnp
from jax import lax
from jax.experimental import pallas as pl
from jax.experimental.pallas import tpu as pltpu

F32 = jnp.float32
BF16 = jnp.bfloat16
RMS_EPS = 1e-6
L2_EPS = 1e-6
LANES = 128
SUBLANES = 8
VMEM_LIMIT_BYTES = 48 * 1024 * 1024

GDN_HEADS = 4
GDN_DK = 128
GDN_CONV = 4
GDN_CHUNK = 128
DIFF_HEADS = 4
DIFF_DQK = 64
XA_HEADS = 4
XA_DH = 128
N_BRANCH = 3
BRANCH_W = 512
PAGES_PER_STEP = 8

COL_QKV, COL_Z, COL_DQ, COL_DK, COL_DV, COL_XQ, COL_GATE = 0, 12, 16, 20, 24, 28, 32
MAIN_COLS = 56 * LANES


def _params(*sem):
    return pltpu.CompilerParams(dimension_semantics=sem, vmem_limit_bytes=VMEM_LIMIT_BYTES)


def _rms(x, g):
    return x * lax.rsqrt(jnp.mean(x * x, axis=-1, keepdims=True) + RMS_EPS) * g


def _sigmoid(x):
    return 1.0 / (1.0 + jnp.exp(-x))


def _dot(a, b, precision=None):
    return jnp.dot(a, b, preferred_element_type=F32, precision=precision)


def _dot_nt(a, b, precision=None):
    return lax.dot_general(a, b, (((1,), (1,)), ((), ())), preferred_element_type=F32, precision=precision)


def _dot_tn(a, b, precision=None):
    return lax.dot_general(a, b, (((0,), (0,)), ((), ())), preferred_element_type=F32, precision=precision)


def _diff_lambda(lam_ref, lam_init):
    lp = lam_ref[...]
    a = jnp.sum(lp[0:1] * lp[1:2], axis=-1, keepdims=True)
    b = jnp.sum(lp[2:3] * lp[3:4], axis=-1, keepdims=True)
    return jnp.exp(a) - jnp.exp(b) + lam_init


def _norm_matmul_kernel(x_ref, g_ref, w_ref, *rest, has_side):
    if has_side:
        ws_ref, o_ref, os_ref, h_ref = rest
    else:
        o_ref, h_ref = rest

    @pl.when(pl.program_id(1) == 0)
    def _():
        h = _rms(x_ref[...], g_ref[...]).astype(BF16)
        h_ref[...] = h
        if has_side:
            os_ref[...] = _dot(h, ws_ref[...])

    o_ref[...] = _dot(h_ref[...], w_ref[...])


def _norm_matmul(x, g, w, w_side=None, *, tm, tn):
    m, d = x.shape
    n = w.shape[1]
    has_side = w_side is not None
    in_specs = [pl.BlockSpec((tm, d), lambda i, j: (i, 0)),
                pl.BlockSpec((1, d), lambda i, j: (0, 0)),
                pl.BlockSpec((d, tn), lambda i, j: (0, j))]
    out_shape = [jax.ShapeDtypeStruct((m, n), F32)]
    out_specs = [pl.BlockSpec((tm, tn), lambda i, j: (i, j))]
    args = [x, g.reshape(1, d), w]
    if has_side:
        ns = w_side.shape[1]
        in_specs.append(pl.BlockSpec((d, ns), lambda i, j: (0, 0)))
        out_shape.append(jax.ShapeDtypeStruct((m, ns), F32))
        out_specs.append(pl.BlockSpec((tm, ns), lambda i, j: (i, 0)))
        args.append(w_side)
    res = pl.pallas_call(
        functools.partial(_norm_matmul_kernel, has_side=has_side),
        grid=(m // tm, n // tn),
        in_specs=in_specs, out_specs=out_specs, out_shape=out_shape,
        scratch_shapes=[pltpu.VMEM((tm, d), BF16)],
        compiler_params=_params("parallel", "arbitrary"),
        name="norm_matmul",
    )(*args)
    return res if has_side else res[0]


def _gdn_kernel(qkv_ref, ab_ref, z_ref, buf_ref, s0_ref, wconv_ref, alog_ref, dtb_ref, gout_ref,
                o_ref, sfin_ref, tail_ref, s_ref, *, chunk, valid_len):
    C = chunk
    c = pl.program_id(1)
    hi = lax.Precision.HIGHEST

    @pl.when(c == 0)
    def _():
        tail_ref[...] = buf_ref[...]
        s_ref[...] = s0_ref[...]

    cur = qkv_ref[...]
    ext = jnp.concatenate([tail_ref[...], cur], axis=0)
    wc = wconv_ref[...]
    first = SUBLANES - (GDN_CONV - 1)
    y = ext[first:first + C] * wc[0:1]
    for i in range(1, GDN_CONV):
        y = y + ext[first + i:first + i + C] * wc[i:i + 1]
    tail_ref[...] = cur[C - SUBLANES:]
    y = y * _sigmoid(y)

    ab = ab_ref[...]
    sp_in = ab + dtb_ref[...]
    softplus = jnp.maximum(sp_in, 0.0) + jnp.log(1.0 + jnp.exp(-jnp.abs(sp_in)))
    g_all = -jnp.exp(alog_ref[...]) * softplus
    beta_all = _sigmoid(ab)
    if valid_len is not None:
        row_ok = (c * C + lax.broadcasted_iota(jnp.int32, (C, 1), 0)) < valid_len
        g_all = jnp.where(row_ok, g_all, 0.0)
        beta_all = jnp.where(row_ok, beta_all, 0.0)

    row = lax.broadcasted_iota(jnp.int32, (C, C), 0)
    col = lax.broadcasted_iota(jnp.int32, (C, C), 1)
    incl = row >= col
    strict = row > col
    tril = jnp.where(incl, 1.0, 0.0).astype(F32)
    eye = jnp.where(row == col, 1.0, 0.0).astype(F32)
    ones8 = jnp.ones((SUBLANES, C), F32)
    g_cum_all = _dot(tril, g_all, hi)
    n_double = max(int(math.ceil(math.log2(C))) - 1, 0)

    for h in range(GDN_HEADS):
        sl = slice(h * GDN_DK, (h + 1) * GDN_DK)
        qh = y[:, sl]
        kh = y[:, GDN_HEADS * GDN_DK + h * GDN_DK:GDN_HEADS * GDN_DK + (h + 1) * GDN_DK]
        vh = y[:, 2 * GDN_HEADS * GDN_DK + h * GDN_DK:2 * GDN_HEADS * GDN_DK + (h + 1) * GDN_DK]
        qh = qh * lax.rsqrt(jnp.sum(qh * qh, axis=-1, keepdims=True) + L2_EPS) * (GDN_DK ** -0.5)
        kh = kh * lax.rsqrt(jnp.sum(kh * kh, axis=-1, keepdims=True) + L2_EPS)
        g_h = g_all[:, h:h + 1]
        beta_h = beta_all[:, GDN_HEADS + h:GDN_HEADS + h + 1]
        g_cum = g_cum_all[:, h:h + 1]
        g_cum_row = _dot(ones8, jnp.where(row <= col, g_h, 0.0), hi)[0:1]
        decay = jnp.where(incl, jnp.exp(jnp.where(incl, g_cum - g_cum_row, 0.0)), 0.0)
        kb = kh * beta_h
        kh16 = kh.astype(BF16)
        lmat = jnp.where(strict, _dot_nt(kb.astype(BF16), kh16) * decay, 0.0)
        pw = -lmat
        tinv = eye + pw
        for _ in range(n_double):
            pw = _dot(pw, pw, hi)
            tinv = tinv + _dot(tinv, pw, hi)
        e_g = jnp.exp(g_cum)
        rhs = jnp.concatenate([vh * beta_h, kb * e_g], axis=1)
        sol = _dot(tinv, rhs, hi)
        u0 = sol[:, :GDN_DK]
        wcd = sol[:, GDN_DK:]
        qk = _dot_nt(qh.astype(BF16), kh16) * decay
        qg = qh * e_g
        g_last = g_cum[C - 1:C]
        kg = kh * jnp.exp(g_last - g_cum)
        s = s_ref[h]
        s16 = s.astype(BF16)
        u = u0 - _dot(wcd.astype(BF16), s16)
        u16 = u.astype(BF16)
        o = _dot(qg.astype(BF16), s16) + _dot(qk.astype(BF16), u16)
        s_ref[h] = s * jnp.exp(g_last) + _dot_tn(kg.astype(BF16), u16)
        zh = z_ref[:, sl]
        o_ref[:, sl] = _rms(o, gout_ref[...]) * (zh * _sigmoid(zh))

    @pl.when(c == pl.num_programs(1) - 1)
    def _():
        sfin_ref[...] = s_ref[...]


def _gdn(xz, ab, buf8, s0, w_conv, a_log, dt_bias, g_out, *, n_seq, seq_rows, valid_len):
    C = GDN_CHUNK
    nc = seq_rows // C
    qkv_w = 3 * GDN_HEADS * GDN_DK
    zeros_h = jnp.zeros((GDN_HEADS,), F32)
    alog8 = jnp.concatenate([a_log, zeros_h]).reshape(1, 2 * GDN_HEADS)
    dtb8 = jnp.concatenate([dt_bias, zeros_h]).reshape(1, 2 * GDN_HEADS)
    state_shape = (n_seq, GDN_HEADS, GDN_DK, GDN_DK)
    return pl.pallas_call(
        functools.partial(_gdn_kernel, chunk=C, valid_len=None if valid_len == seq_rows else valid_len),
        grid=(n_seq, nc),
        in_specs=[
            pl.BlockSpec((C, qkv_w), lambda b, c: (b * nc + c, 0)),
            pl.BlockSpec((C, 2 * GDN_HEADS), lambda b, c: (b * nc + c, 0)),
            pl.BlockSpec((C, BRANCH_W), lambda b, c: (b * nc + c, COL_Z * LANES // BRANCH_W)),
            pl.BlockSpec((None, SUBLANES, qkv_w), lambda b, c: (b, 0, 0)),
            pl.BlockSpec((None, GDN_HEADS, GDN_DK, GDN_DK), lambda b, c: (b, 0, 0, 0)),
            pl.BlockSpec((GDN_CONV, qkv_w), lambda b, c: (0, 0)),
            pl.BlockSpec((1, 2 * GDN_HEADS), lambda b, c: (0, 0)),
            pl.BlockSpec((1, 2 * GDN_HEADS), lambda b, c: (0, 0)),
            pl.BlockSpec((1, GDN_DK), lambda b, c: (0, 0)),
        ],
        out_specs=[
            pl.BlockSpec((C, BRANCH_W), lambda b, c: (b * nc + c, 0)),
            pl.BlockSpec((None, GDN_HEADS, GDN_DK, GDN_DK), lambda b, c: (b, 0, 0, 0)),
        ],
        out_shape=[jax.ShapeDtypeStruct((n_seq * seq_rows, BRANCH_W), F32),
                   jax.ShapeDtypeStruct(state_shape, F32)],
        scratch_shapes=[pltpu.VMEM((SUBLANES, qkv_w), F32),
                        pltpu.VMEM((GDN_HEADS, GDN_DK, GDN_DK), F32)],
        compiler_params=_params("parallel", "arbitrary"),
        name="gdn",
    )(xz, ab, xz, buf8, s0, w_conv, alog8, dtb8, g_out.reshape(1, GDN_DK))


def _split_components(q):
    lane = lax.broadcasted_iota(jnp.int32, q.shape, 1)
    return jnp.concatenate([jnp.where(lane < DIFF_DQK, q, 0.0), jnp.where(lane >= DIFF_DQK, q, 0.0)], axis=0)


def _diff_prompt_kernel(lam_ref, q_ref, k_ref, v_ref, g_ref, o_ref, m_ref, l_ref, acc_ref, *, tq, lam_init):
    i = pl.program_id(2)
    qbd = _split_components(q_ref[...] * (DIFF_DQK ** -0.5)).astype(BF16)
    m_ref[...] = jnp.full(m_ref.shape, -jnp.inf, F32)
    l_ref[...] = jnp.zeros(l_ref.shape, F32)
    acc_ref[...] = jnp.zeros(acc_ref.shape, F32)

    def step(j, masked):
        start = pl.multiple_of(j * tq, tq)
        k = k_ref[pl.ds(start, tq), :].astype(BF16)
        v = v_ref[pl.ds(start, tq), :].astype(BF16)
        s = _dot_nt(qbd, k)
        if masked:
            r = lax.broadcasted_iota(jnp.int32, (tq, tq), 0)
            cidx = lax.broadcasted_iota(jnp.int32, (tq, tq), 1)
            keep = jnp.concatenate([cidx <= r, cidx <= r], axis=0)
            s = jnp.where(keep, s, -jnp.inf)
        m_old = m_ref[...]
        m_new = jnp.maximum(m_old, jnp.max(s, axis=-1, keepdims=True))
        alpha = jnp.exp(m_old - m_new)
        p = jnp.exp(s - m_new)
        l_ref[...] = l_ref[...] * alpha + jnp.sum(p, axis=-1, keepdims=True)
        acc_ref[...] = acc_ref[...] * alpha + _dot(p.astype(BF16), v)
        m_ref[...] = m_new

    def body(j, carry):
        step(j, False)
        return carry

    lax.fori_loop(0, i, body, 0)
    step(i, True)
    o = acc_ref[...] / l_ref[...]
    lam = _diff_lambda(lam_ref, lam_init)
    a = o[:tq] - lam * o[tq:]
    o_ref[...] = _rms(a, g_ref[...]) * (1.0 - lam_init)


def _diff_prompt(proj, lam_p, g_sub, *, n_seq, seq, tq, lam_init):
    nq = seq // tq
    dv = 2 * DIFF_DQK
    return pl.pallas_call(
        functools.partial(_diff_prompt_kernel, tq=tq, lam_init=lam_init),
        grid=(n_seq, DIFF_HEADS, nq),
        in_specs=[
            pl.BlockSpec((4, DIFF_DQK), lambda b, h, i: (0, 0)),
            pl.BlockSpec((tq, LANES), lambda b, h, i: (b * nq + i, COL_DQ + h)),
            pl.BlockSpec((seq, LANES), lambda b, h, i: (b, COL_DK + h)),
            pl.BlockSpec((seq, LANES), lambda b, h, i: (b, COL_DV + h)),
            pl.BlockSpec((1, dv), lambda b, h, i: (0, 0)),
        ],
        out_specs=pl.BlockSpec((tq, dv), lambda b, h, i: (b * nq + i, h)),
        out_shape=jax.ShapeDtypeStruct((n_seq * seq, DIFF_HEADS * dv), F32),
        scratch_shapes=[pltpu.VMEM((2 * tq, 1), F32), pltpu.VMEM((2 * tq, 1), F32),
                        pltpu.VMEM((2 * tq, dv), F32)],
        compiler_params=_params("parallel", "parallel", "arbitrary"),
        name="diff_attn_prompt",
    )(lam_p, proj, proj, proj, g_sub.reshape(1, dv))


def _diff_paged_kernel(pt_ref, lam_ref, q_ref, ks_ref, vs_ref, g_ref, *rest, t_new, n_pg, lam_init):
    del pt_ref
    k_pages = rest[:n_pg]
    v_pages = rest[n_pg:2 * n_pg]
    o_ref, qbd_ref, m_ref, l_ref, acc_ref = rest[2 * n_pg:]
    T = t_new
    R = 2 * T
    dv = 2 * DIFF_DQK
    step_idx = pl.program_id(1)

    @pl.when(step_idx == 0)
    def _():
        q = q_ref[...] * (DIFF_DQK ** -0.5)
        t_of_row = lax.broadcasted_iota(jnp.int32, (R, 1), 0) % T
        for h in range(DIFF_HEADS):
            sl = slice(h * dv, (h + 1) * dv)
            rows = slice(h * R, (h + 1) * R)
            qbd = _split_components(q[:, sl])
            qbd_ref[rows, :] = qbd.astype(BF16)
            ks = ks_ref[:, sl]
            vs = vs_ref[:, sl]
            scores = []
            for j in range(T):
                sj = jnp.sum(qbd * ks[j:j + 1], axis=-1, keepdims=True)
                scores.append(jnp.where(t_of_row >= j, sj, -jnp.inf))
            m = scores[0]
            for j in range(1, T):
                m = jnp.maximum(m, scores[j])
            den = jnp.zeros((R, 1), F32)
            acc = jnp.zeros((R, dv), F32)
            for j in range(T):
                pj = jnp.exp(scores[j] - m)
                den = den + pj
                acc = acc + pj * vs[j:j + 1]
            m_ref[rows, :] = m
            l_ref[rows, :] = den
            acc_ref[rows, :] = acc

    for h in range(DIFF_HEADS):
        sl = slice(h * dv, (h + 1) * dv)
        rows = slice(h * R, (h + 1) * R)
        qbd = qbd_ref[rows, :]
        s = jnp.concatenate([_dot_nt(qbd, kp[:, sl].astype(BF16)) for kp in k_pages], axis=1)
        m_old = m_ref[rows, :]
        m_new = jnp.maximum(m_old, jnp.max(s, axis=-1, keepdims=True))
        alpha = jnp.exp(m_old - m_new)
        p = jnp.exp(s - m_new)
        l_ref[rows, :] = l_ref[rows, :] * alpha + jnp.sum(p, axis=-1, keepdims=True)
        p16 = p.astype(BF16)
        page = k_pages[0].shape[0]
        pv = _dot(p16[:, :page], v_pages[0][:, sl].astype(BF16))
        for n in range(1, n_pg):
            pv = pv + _dot(p16[:, n * page:(n + 1) * page], v_pages[n][:, sl].astype(BF16))
        acc_ref[rows, :] = acc_ref[rows, :] * alpha + pv
        m_ref[rows, :] = m_new

    @pl.when(step_idx == pl.num_programs(1) - 1)
    def _():
        lam = _diff_lambda(lam_ref, lam_init)
        o = acc_ref[...] / l_ref[...]
        for h in range(DIFF_HEADS):
            a = o[h * R:h * R + T] - lam * o[h * R + T:(h + 1) * R]
            o_ref[:, h * dv:(h + 1) * dv] = _rms(a, g_ref[...]) * (1.0 - lam_init)


def _diff_paged(proj, lam_p, g_sub, cache_k, cache_v, page_table, *, layer, t_new, lam_init):
    depth, n_pool, page, heads, kw = cache_k.shape
    n_seq, n_pages = page_table.shape
    n_pg = PAGES_PER_STEP
    n_steps = n_pages // n_pg
    dv = 2 * DIFF_DQK
    w = heads * kw
    ck = cache_k.reshape(depth * n_pool, page, w)
    cv = cache_v.reshape(depth * n_pool, page, heads * cache_v.shape[-1])
    pt = page_table.reshape(-1)
    base = layer * n_pool

    def page_spec(n):
        return pl.BlockSpec((None, page, w), lambda b, s, pt: (base + pt[b * n_pages + s * n_pg + n], 0, 0))

    blk = w // (DIFF_HEADS * dv)
    del blk
    grid_spec = pltpu.PrefetchScalarGridSpec(
        num_scalar_prefetch=1,
        grid=(n_seq, n_steps),
        in_specs=[
            pl.BlockSpec((4, DIFF_DQK), lambda b, s, pt: (0, 0)),
            pl.BlockSpec((t_new, w), lambda b, s, pt: (b, COL_DQ * LANES // w)),
            pl.BlockSpec((t_new, w), lambda b, s, pt: (b, COL_DK * LANES // w)),
            pl.BlockSpec((t_new, w), lambda b, s, pt: (b, COL_DV * LANES // w)),
            pl.BlockSpec((1, dv), lambda b, s, pt: (0, 0)),
        ] + [page_spec(n) for n in range(n_pg)] + [page_spec(n) for n in range(n_pg)],
        out_specs=pl.BlockSpec((t_new, w), lambda b, s, pt: (b, 0)),
        scratch_shapes=[pltpu.VMEM((DIFF_HEADS * 2 * t_new, dv), BF16),
                        pltpu.VMEM((DIFF_HEADS * 2 * t_new, 1), F32),
                        pltpu.VMEM((DIFF_HEADS * 2 * t_new, 1), F32),
                        pltpu.VMEM((DIFF_HEADS * 2 * t_new, dv), F32)],
    )
    return pl.pallas_call(
        functools.partial(_diff_paged_kernel, t_new=t_new, n_pg=n_pg, lam_init=lam_init),
        grid_spec=grid_spec,
        out_shape=jax.ShapeDtypeStruct((n_seq * t_new, w), F32),
        compiler_params=_params("parallel", "arbitrary"),
        name="diff_attn_paged",
    )(pt, lam_p, proj, proj, proj, g_sub.reshape(1, dv), *([ck] * n_pg), *([cv] * n_pg))


def _cross_kernel(q_ref, mk_ref, mv_ref, o_ref):
    q = q_ref[...] * (XA_DH ** -0.5)
    for h in range(XA_HEADS):
        sl = slice(h * XA_DH, (h + 1) * XA_DH)
        s = _dot_nt(q[:, sl].astype(BF16), mk_ref[:, sl].astype(BF16))
        p = jnp.exp(s - jnp.max(s, axis=-1, keepdims=True))
        den = jnp.sum(p, axis=-1, keepdims=True)
        o_ref[:, sl] = _dot(p.astype(BF16), mv_ref[:, sl].astype(BF16)) / den


def _cross_attn(proj, mk, mv, mk_col, mv_col, *, n_seq, seq, n_mem, tq):
    nq = seq // tq
    w = XA_HEADS * XA_DH
    return pl.pallas_call(
        _cross_kernel,
        grid=(n_seq, nq),
        in_specs=[
            pl.BlockSpec((tq, w), lambda b, i: (b * nq + i, COL_XQ * LANES // w)),
            pl.BlockSpec((n_mem, w), lambda b, i: (b, mk_col)),
            pl.BlockSpec((n_mem, w), lambda b, i: (b, mv_col)),
        ],
        out_specs=pl.BlockSpec((tq, w), lambda b, i: (b * nq + i, 0)),
        out_shape=jax.ShapeDtypeStruct((n_seq * seq, w), F32),
        compiler_params=_params("parallel", "arbitrary"),
        name="cross_attn",
    )(proj, mk, mv)


def _merge_kernel(x_ref, oa_ref, ob_ref, oc_ref, ga_ref, gb_ref, gc_ref, wb_ref, wo_ref, g_ref, o_ref):
    merged = None
    for n, (br, gl) in enumerate(((oa_ref, ga_ref), (ob_ref, gb_ref), (oc_ref, gc_ref))):
        up = _dot(br[...].astype(BF16), wb_ref[n])
        term = _sigmoid(gl[...]) * up
        merged = term if merged is None else merged + term
    y = _dot(merged.astype(BF16), wo_ref[...])
    o_ref[...] = x_ref[...] + _rms(y, g_ref[...])


def _merge(x, oa, ob, oc, proj, w_branch, w_out, g_post, *, tm):
    m, d = x.shape
    gate0 = COL_GATE * LANES // d
    row = lambda i: (i, 0)
    return pl.pallas_call(
        _merge_kernel,
        grid=(m // tm,),
        in_specs=[
            pl.BlockSpec((tm, d), row),
            pl.BlockSpec((tm, BRANCH_W), row), pl.BlockSpec((tm, BRANCH_W), row), pl.BlockSpec((tm, BRANCH_W), row),
            pl.BlockSpec((tm, d), lambda i: (i, gate0)),
            pl.BlockSpec((tm, d), lambda i: (i, gate0 + 1)),
            pl.BlockSpec((tm, d), lambda i: (i, gate0 + 2)),
            pl.BlockSpec((N_BRANCH, BRANCH_W, d), lambda i: (0, 0, 0)),
            pl.BlockSpec((d, d), lambda i: (0, 0)),
            pl.BlockSpec((1, d), lambda i: (0, 0)),
        ],
        out_specs=pl.BlockSpec((tm, d), row),
        out_shape=jax.ShapeDtypeStruct((m, d), F32),
        compiler_params=_params("parallel"),
        name="merge",
    )(x, oa, ob, oc, proj, proj, proj, w_branch, w_out, g_post.reshape(1, d))


def _mlp_kernel(x_ref, gpre_ref, w1_ref, w2_ref, gpost_ref, o_ref, h_ref, acc_ref):
    j = pl.program_id(1)

    @pl.when(j == 0)
    def _():
        h_ref[...] = _rms(x_ref[...], gpre_ref[...]).astype(BF16)
        acc_ref[...] = jnp.zeros(acc_ref.shape, F32)

    a = jnp.square(jnp.maximum(_dot(h_ref[...], w1_ref[...]), 0.0))
    acc_ref[...] += _dot(a.astype(BF16), w2_ref[...])

    @pl.when(j == pl.num_programs(1) - 1)
    def _():
        o_ref[...] = x_ref[...] + _rms(acc_ref[...], gpost_ref[...])


def _mlp(x, g_pre, w1, w2, g_post, *, tm, tf):
    m, d = x.shape
    dff = w1.shape[1]
    return pl.pallas_call(
        _mlp_kernel,
        grid=(m // tm, dff // tf),
        in_specs=[
            pl.BlockSpec((tm, d), lambda i, j: (i, 0)),
            pl.BlockSpec((1, d), lambda i, j: (0, 0)),
            pl.BlockSpec((d, tf), lambda i, j: (0, j)),
            pl.BlockSpec((tf, d), lambda i, j: (j, 0)),
            pl.BlockSpec((1, d), lambda i, j: (0, 0)),
        ],
        out_specs=pl.BlockSpec((tm, d), lambda i, j: (i, 0)),
        out_shape=jax.ShapeDtypeStruct((m, d), F32),
        scratch_shapes=[pltpu.VMEM((tm, d), BF16), pltpu.VMEM((tm, d), F32)],
        compiler_params=_params("parallel", "arbitrary"),
        name="mlp",
    )(x, g_pre.reshape(1, d), w1, w2, g_post.reshape(1, d))


def _row_tile(m, cap):
    t = min(m, cap)
    assert m % t == 0
    return t


def _trunk_layer(x, w, lam_init, *, n_seq, seq, buf8, s0, mem, attend):
    m, d = x.shape
    tm = _row_tile(m, 1024)
    proj, ab = _norm_matmul(x, w["g_pre_mix"], w["w_main"], w["w_ab"], tm=tm, tn=1024)

    C = GDN_CHUNK
    seq_pad = -(-seq // C) * C
    if seq_pad == seq:
        xz, ab_p = proj, ab
    else:
        xz = jnp.pad(proj[:, :(COL_Z + 4) * LANES].reshape(n_seq, seq, -1), ((0, 0), (0, seq_pad - seq), (0, 0)))
        xz = xz.reshape(n_seq * seq_pad, -1)
        ab_p = jnp.pad(ab.reshape(n_seq, seq, -1), ((0, 0), (0, seq_pad - seq), (0, 0))).reshape(n_seq * seq_pad, -1)
    o_a, s_new = _gdn(xz, ab_p, buf8, s0, w["w_conv"], w["gdn_a_log"], w["gdn_dt_bias"], w["g_gdn_out"],
                      n_seq=n_seq, seq_rows=seq_pad, valid_len=seq)
    if seq_pad != seq:
        o_a = o_a.reshape(n_seq, seq_pad, -1)[:, :seq].reshape(m, -1)

    o_b = attend(proj)
    mk, mv, mk_col, mv_col, n_mem = mem
    o_c = _cross_attn(proj, mk, mv, mk_col, mv_col, n_seq=n_seq, seq=seq, n_mem=n_mem, tq=_row_tile(seq, 512))

    tmm = _row_tile(m, 512)
    x = _merge(x, o_a, o_b, o_c, proj, w["w_branch"], w["w_out"], w["g_post_mix"], tm=tmm)
    x = _mlp(x, w["g_pre_mlp"], w["w_ff1"], w["w_ff2"], w["g_post_mlp"], tm=tmm, tf=1024)
    return x, proj, s_new


def kernel(x_prompt, x_sample, mem_prompt, cache_diff_k, cache_diff_v, page_table, state_gdn, cache_gdn_conv,
           cache_mem_k, cache_mem_v, g_pre_mix, w_in, w_conv, gdn_a_log, gdn_dt_bias, g_gdn_out, diff_lambda,
           g_diff_sub, g_mem, w_mem_k, w_mem_v, w_branch, w_out, g_post_mix, g_pre_mlp, w_ff1, w_ff2, g_post_mlp):
    bp, sp, d = x_prompt.shape
    bs, ts, _ = x_sample.shape
    depth = w_in.shape[0]
    n_mem = mem_prompt.shape[1]
    qkv_w = 3 * GDN_HEADS * GDN_DK
    assert sp >= GDN_CONV - 1 and ts >= GDN_CONV - 1
    xa_w = XA_HEADS * XA_DH
    dk_w = DIFF_HEADS * 2 * DIFF_DQK

    xp = x_prompt.reshape(bp * sp, d)
    xs = x_sample.reshape(bs * ts, d)
    memf = mem_prompt.reshape(bp * n_mem, d)

    c_z = qkv_w
    c_a = c_z + BRANCH_W
    c_b = c_a + GDN_HEADS
    c_dq = c_b + GDN_HEADS

    outs = {k: [] for k in ("kp", "vp", "sp", "cp", "mk", "mv", "ks", "vs", "ss", "cs")}
    zero_buf = jnp.zeros((bp, SUBLANES, qkv_w), F32)
    zero_state = jnp.zeros((bp, GDN_HEADS, GDN_DK, GDN_DK), F32)
    for l in range(depth):
        lam_init = 0.8 - 0.6 * math.exp(-0.3 * l)
        w = {
            "g_pre_mix": g_pre_mix[l],
            "w_main": jnp.concatenate([w_in[l][:, :c_a], w_in[l][:, c_dq:]], axis=1).astype(BF16),
            "w_ab": w_in[l][:, c_a:c_dq].astype(BF16),
            "w_conv": w_conv[l], "gdn_a_log": gdn_a_log[l], "gdn_dt_bias": gdn_dt_bias[l],
            "g_gdn_out": g_gdn_out[l],
            "w_branch": w_branch[l].astype(BF16), "w_out": w_out[l].astype(BF16), "g_post_mix": g_post_mix[l],
            "g_pre_mlp": g_pre_mlp[l], "w_ff1": w_ff1[l].astype(BF16), "w_ff2": w_ff2[l].astype(BF16),
            "g_post_mlp": g_post_mlp[l],
        }
        lam_p = diff_lambda[l]
        g_sub = g_diff_sub[l]

        w_mem = jnp.concatenate([w_mem_k[l], w_mem_v[l]], axis=1).astype(BF16)
        memkv = _norm_matmul(memf, g_mem[l], w_mem, tm=_row_tile(bp * n_mem, 1024), tn=xa_w)
        attend_p = functools.partial(_diff_prompt, lam_p=lam_p, g_sub=g_sub, n_seq=bp, seq=sp,
                                     tq=_row_tile(sp, 256), lam_init=lam_init)
        xp, proj_p, s_p = _trunk_layer(xp, w, lam_init, n_seq=bp, seq=sp, buf8=zero_buf, s0=zero_state,
                                       mem=(memkv, memkv, 0, 1, n_mem), attend=attend_p)
        proj3 = proj_p.reshape(bp, sp, -1)
        outs["kp"].append(proj3[:, :, COL_DK * LANES:COL_DK * LANES + dk_w].reshape(bp, sp, DIFF_HEADS, -1))
        outs["vp"].append(proj3[:, :, COL_DV * LANES:COL_DV * LANES + dk_w].reshape(bp, sp, DIFF_HEADS, -1))
        outs["sp"].append(s_p)
        outs["cp"].append(proj3[:, sp - (GDN_CONV - 1):, :qkv_w])
        outs["mk"].append(memkv[:, :xa_w].reshape(bp, n_mem, XA_HEADS, XA_DH))
        outs["mv"].append(memkv[:, xa_w:].reshape(bp, n_mem, XA_HEADS, XA_DH))

        buf8 = jnp.pad(cache_gdn_conv[l], ((0, 0), (SUBLANES - (GDN_CONV - 1), 0), (0, 0)))
        attend_s = functools.partial(_diff_paged, lam_p=lam_p, g_sub=g_sub, cache_k=cache_diff_k,
                                     cache_v=cache_diff_v, page_table=page_table, layer=l, t_new=ts,
                                     lam_init=lam_init)
        mk_s = cache_mem_k[l].reshape(bs * n_mem, xa_w)
        mv_s = cache_mem_v[l].reshape(bs * n_mem, xa_w)
        xs, proj_s, s_s = _trunk_layer(xs, w, lam_init, n_seq=bs, seq=ts, buf8=buf8, s0=state_gdn[l],
                                       mem=(mk_s, mv_s, 0, 0, n_mem), attend=attend_s)
        proj3 = proj_s.reshape(bs, ts, -1)
        outs["ks"].append(proj3[:, :, COL_DK * LANES:COL_DK * LANES + dk_w].reshape(bs, ts, DIFF_HEADS, -1))
        outs["vs"].append(proj3[:, :, COL_DV * LANES:COL_DV * LANES + dk_w].reshape(bs, ts, DIFF_HEADS, -1))
        outs["ss"].append(s_s)
        outs["cs"].append(proj3[:, ts - (GDN_CONV - 1):, :qkv_w])

    st = lambda k: jnp.stack(outs[k])
    return (xp.reshape(bp, sp, d), xs.reshape(bs, ts, d),
            st("kp"), st("vp"), st("sp"), st("cp"), st("mk"), st("mv"),
            st("ks"), st("vs"), st("ss"), st("cs"))
```

```python
import functools
import math

import jax
import jax.numpy as jnp
from jax import lax
from jax.experimental import pallas as pl
from jax.experimental.pallas import tpu as pltpu

F32 = jnp.float32
BF16 = jnp.bfloat16
RMS_EPS = 1e-6
L2_EPS = 1e-6
LANES = 128
SUBLANES = 8
VMEM_LIMIT_BYTES = 48 * 1024 * 1024

GDN_HEADS = 4
GDN_DK = 128
GDN_CONV = 4
GDN_CHUNK = 128
DIFF_HEADS = 4
DIFF_DQK = 64
XA_HEADS = 4
XA_DH = 128
N_BRANCH = 3
BRANCH_W = 512
PAGES_PER_STEP = 16
PAGE_GROUP = 4
LOG2_E = math.log2(math.e)

COL_QKV, COL_Z, COL_DQ, COL_DK, COL_DV, COL_XQ, COL_GATE = 0, 12, 16, 20, 24, 28, 32
MAIN_COLS = 56 * LANES


def _params(*sem):
    return pltpu.CompilerParams(dimension_semantics=sem, vmem_limit_bytes=VMEM_LIMIT_BYTES)


def _rms(x, g):
    return x * lax.rsqrt(jnp.mean(x * x, axis=-1, keepdims=True) + RMS_EPS) * g


def _sigmoid(x):
    return 1.0 / (1.0 + jnp.exp(-x))


def _dot(a, b, precision=None):
    return jnp.dot(a, b, preferred_element_type=F32, precision=precision)


def _dot_nt(a, b, precision=None):
    return lax.dot_general(a, b, (((1,), (1,)), ((), ())), preferred_element_type=F32, precision=precision)


def _dot_tn(a, b, precision=None):
    return lax.dot_general(a, b, (((0,), (0,)), ((), ())), preferred_element_type=F32, precision=precision)


def _unit_lower_inverses(lmats, eye, level):
    C = eye.shape[0]
    ts = [eye - jnp.where(level <= 1, lm, 0.0) for lm in lmats]
    for k in range(2, C.bit_length()):
        t16s = [t.astype(BF16) for t in ts]
        xs = [_dot(jnp.where(level == k, lm, 0.0).astype(BF16), t16) for lm, t16 in zip(lmats, t16s)]
        ts = [t - _dot(t16, x.astype(BF16)) for t, t16, x in zip(ts, t16s, xs)]
    return ts


def _diff_lambda(lam_ref, lam_init):
    lp = lam_ref[...]
    a = jnp.sum(lp[0:1] * lp[1:2], axis=-1, keepdims=True)
    b = jnp.sum(lp[2:3] * lp[3:4], axis=-1, keepdims=True)
    return jnp.exp(a) - jnp.exp(b) + lam_init


def _norm_matmul_kernel(x_ref, g_ref, w_ref, *rest, has_side):
    if has_side:
        ws_ref, o_ref, os_ref, h_ref = rest
    else:
        o_ref, h_ref = rest

    @pl.when(pl.program_id(1) == 0)
    def _():
        h = _rms(x_ref[...], g_ref[...]).astype(BF16)
        h_ref[...] = h
        if has_side:
            os_ref[...] = _dot(h, ws_ref[...])

    o_ref[...] = _dot(h_ref[...], w_ref[...])


def _norm_matmul(x, g, w, w_side=None, *, tm, tn):
    m, d = x.shape
    n = w.shape[1]
    has_side = w_side is not None
    in_specs = [pl.BlockSpec((tm, d), lambda i, j: (i, 0)),
                pl.BlockSpec((1, d), lambda i, j: (0, 0)),
                pl.BlockSpec((d, tn), lambda i, j: (0, j))]
    out_shape = [jax.ShapeDtypeStruct((m, n), F32)]
    out_specs = [pl.BlockSpec((tm, tn), lambda i, j: (i, j))]
    args = [x, g.reshape(1, d), w]
    if has_side:
        ns = w_side.shape[1]
        in_specs.append(pl.BlockSpec((d, ns), lambda i, j: (0, 0)))
        out_shape.append(jax.ShapeDtypeStruct((m, ns), F32))
        out_specs.append(pl.BlockSpec((tm, ns), lambda i, j: (i, 0)))
        args.append(w_side)
    res = pl.pallas_call(
        functools.partial(_norm_matmul_kernel, has_side=has_side),
        grid=(m // tm, n // tn),
        in_specs=in_specs, out_specs=out_specs, out_shape=out_shape,
        scratch_shapes=[pltpu.VMEM((tm, d), BF16)],
        compiler_params=_params("parallel", "arbitrary"),
        name="norm_matmul",
    )(*args)
    return res if has_side else res[0]


def _gdn_kernel(qkv_ref, ab_ref, z_ref, buf_ref, s0_ref, wconv_ref, alog_ref, dtb_ref, gout_ref,
                o_ref, sfin_ref, ext_ref, s_ref, *, chunk, valid_len):
    C = chunk
    H = GDN_HEADS
    DK = GDN_DK
    c = pl.program_id(1)
    hi = lax.Precision.HIGHEST

    @pl.when(c == 0)
    def _():
        ext_ref[0:SUBLANES, :] = buf_ref[...]
        s_ref[...] = s0_ref[...]

    ext_ref[SUBLANES:SUBLANES + C, :] = qkv_ref[...]
    wc = wconv_ref[...]
    first = SUBLANES - (GDN_CONV - 1)
    y = ext_ref[first:first + C, :] * wc[0:1]
    for i in range(1, GDN_CONV):
        y = y + ext_ref[first + i:first + i + C, :] * wc[i:i + 1]
    ext_ref[0:SUBLANES, :] = ext_ref[C:C + SUBLANES, :]
    y = y * _sigmoid(y)

    ab = ab_ref[...]
    sp_in = ab + dtb_ref[...]
    softplus = jnp.maximum(sp_in, 0.0) + jnp.log(1.0 + jnp.exp(-jnp.abs(sp_in)))
    g_all = -jnp.exp(alog_ref[...]) * softplus
    beta_all = _sigmoid(ab)
    if valid_len is not None:
        row_ok = (c * C + lax.broadcasted_iota(jnp.int32, (C, 1), 0)) < valid_len
        g_all = jnp.where(row_ok, g_all, 0.0)
        beta_all = jnp.where(row_ok, beta_all, 0.0)

    row = lax.broadcasted_iota(jnp.int32, (C, C), 0)
    col = lax.broadcasted_iota(jnp.int32, (C, C), 1)
    incl = row >= col
    strict = row > col
    tril = jnp.where(incl, 1.0, 0.0).astype(F32)
    eye = jnp.where(row == col, 1.0, 0.0).astype(F32)
    level = 32 - lax.clz(row ^ col)
    g_cum_all = _dot(tril, g_all, hi)

    heads = range(H)
    q = [y[:, h * DK:(h + 1) * DK] for h in heads]
    k = [y[:, (H + h) * DK:(H + h + 1) * DK] for h in heads]
    v = [y[:, (2 * H + h) * DK:(2 * H + h + 1) * DK] for h in heads]
    q = [x * lax.rsqrt(jnp.sum(x * x, axis=-1, keepdims=True) + L2_EPS) * (DK ** -0.5) for x in q]
    k = [x * lax.rsqrt(jnp.sum(x * x, axis=-1, keepdims=True) + L2_EPS) for x in k]
    beta = [beta_all[:, H + h:H + h + 1] for h in heads]
    g_cum = [g_cum_all[:, h:h + 1] for h in heads]
    decay = []
    for h in heads:
        g_cum_b = jnp.broadcast_to(g_cum[h], (C, C))
        decay.append(jnp.where(incl, jnp.exp(jnp.where(incl, g_cum_b - g_cum_b.T, 0.0)), 0.0))
    kb = [k[h] * beta[h] for h in heads]
    k16 = [x.astype(BF16) for x in k]
    lmat = [jnp.where(strict, _dot_nt(kb[h].astype(BF16), k16[h]) * decay[h], 0.0) for h in heads]
    qk = [(_dot_nt(q[h].astype(BF16), k16[h]) * decay[h]).astype(BF16) for h in heads]
    tinv = _unit_lower_inverses(lmat, eye, level)
    e_g = [jnp.exp(g_cum[h]) for h in heads]
    rhs = [jnp.concatenate([v[h] * beta[h], kb[h] * e_g[h]], axis=1).astype(BF16) for h in heads]
    sol = [_dot(tinv[h].astype(BF16), rhs[h]) for h in heads]
    g_last = [g_cum[h][C - 1:C] for h in heads]
    qg = [(q[h] * e_g[h]).astype(BF16) for h in heads]
    kg = [(k[h] * jnp.exp(g_last[h] - g_cum[h])).astype(BF16) for h in heads]
    s = [s_ref[h] for h in heads]
    s16 = [x.astype(BF16) for x in s]
    u = [sol[h][:, :DK] - _dot(sol[h][:, DK:].astype(BF16), s16[h]) for h in heads]
    u16 = [x.astype(BF16) for x in u]
    o = [_dot(qg[h], s16[h]) + _dot(qk[h], u16[h]) for h in heads]
    s_new = [s[h] * jnp.exp(g_last[h]) + _dot_tn(kg[h], u16[h]) for h in heads]
    for h in heads:
        s_ref[h] = s_new[h]
        zh = z_ref[:, h * DK:(h + 1) * DK]
        o_ref[:, h * DK:(h + 1) * DK] = _rms(o[h], gout_ref[...]) * (zh * _sigmoid(zh))

    @pl.when(c == pl.num_programs(1) - 1)
    def _():
        sfin_ref[...] = s_ref[...]


def _gdn(xz, ab, buf8, s0, w_conv, a_log, dt_bias, g_out, *, n_seq, seq_rows, valid_len):
    C = GDN_CHUNK
    nc = seq_rows // C
    qkv_w = 3 * GDN_HEADS * GDN_DK
    zeros_h = jnp.zeros((GDN_HEADS,), F32)
    alog8 = jnp.concatenate([a_log, zeros_h]).reshape(1, 2 * GDN_HEADS)
    dtb8 = jnp.concatenate([dt_bias, zeros_h]).reshape(1, 2 * GDN_HEADS)
    state_shape = (n_seq, GDN_HEADS, GDN_DK, GDN_DK)
    return pl.pallas_call(
        functools.partial(_gdn_kernel, chunk=C, valid_len=None if valid_len == seq_rows else valid_len),
        grid=(n_seq, nc),
        in_specs=[
            pl.BlockSpec((C, qkv_w), lambda b, c: (b * nc + c, 0)),
            pl.BlockSpec((C, 2 * GDN_HEADS), lambda b, c: (b * nc + c, 0)),
            pl.BlockSpec((C, BRANCH_W), lambda b, c: (b * nc + c, COL_Z * LANES // BRANCH_W)),
            pl.BlockSpec((None, SUBLANES, qkv_w), lambda b, c: (b, 0, 0)),
            pl.BlockSpec((None, GDN_HEADS, GDN_DK, GDN_DK), lambda b, c: (b, 0, 0, 0)),
            pl.BlockSpec((GDN_CONV, qkv_w), lambda b, c: (0, 0)),
            pl.BlockSpec((1, 2 * GDN_HEADS), lambda b, c: (0, 0)),
            pl.BlockSpec((1, 2 * GDN_HEADS), lambda b, c: (0, 0)),
            pl.BlockSpec((1, GDN_DK), lambda b, c: (0, 0)),
        ],
        out_specs=[
            pl.BlockSpec((C, BRANCH_W), lambda b, c: (b * nc + c, 0)),
            pl.BlockSpec((None, GDN_HEADS, GDN_DK, GDN_DK), lambda b, c: (b, 0, 0, 0)),
        ],
        out_shape=[jax.ShapeDtypeStruct((n_seq * seq_rows, BRANCH_W), F32),
                   jax.ShapeDtypeStruct(state_shape, F32)],
        scratch_shapes=[pltpu.VMEM((SUBLANES + C, qkv_w), F32),
                        pltpu.VMEM((GDN_HEADS, GDN_DK, GDN_DK), F32)],
        compiler_params=_params("parallel", "arbitrary"),
        name="gdn",
    )(xz, ab, xz, buf8, s0, w_conv, alog8, dtb8, g_out.reshape(1, GDN_DK))


def _split_components(q):
    lane = lax.broadcasted_iota(jnp.int32, q.shape, 1)
    return jnp.concatenate([jnp.where(lane < DIFF_DQK, q, 0.0), jnp.where(lane >= DIFF_DQK, q, 0.0)], axis=0)


def _diff_prompt_kernel(lam_ref, q_ref, k_ref, v_ref, g_ref, o_ref,
                        k16_ref, v16_ref, qbd_ref, m_ref, l_ref, acc_ref, *, tq, lam_init):
    i = pl.program_id(1)
    dv = 2 * DIFF_DQK

    @pl.when(i == 0)
    def _():
        k16_ref[...] = k_ref[...].astype(BF16)
        v16_ref[...] = v_ref[...].astype(BF16)

    q = q_ref[...] * (DIFF_DQK ** -0.5 * LOG2_E)
    for h in range(DIFF_HEADS):
        qbd_ref[h] = _split_components(q[:, h * dv:(h + 1) * dv]).astype(BF16)
    m_ref[...] = jnp.full(m_ref.shape, -jnp.inf, F32)
    l_ref[...] = jnp.zeros(l_ref.shape, F32)
    acc_ref[...] = jnp.zeros(acc_ref.shape, F32)

    def step(j, masked):
        start = pl.multiple_of(j * tq, tq)
        for h in range(DIFF_HEADS):
            sl = slice(h * dv, (h + 1) * dv)
            k = k16_ref[pl.ds(start, tq), sl]
            v = v16_ref[pl.ds(start, tq), sl]
            s = _dot_nt(qbd_ref[h], k)
            if masked:
                r = lax.broadcasted_iota(jnp.int32, (tq, tq), 0)
                cidx = lax.broadcasted_iota(jnp.int32, (tq, tq), 1)
                keep = jnp.concatenate([cidx <= r, cidx <= r], axis=0)
                s = jnp.where(keep, s, -jnp.inf)
            m_old = m_ref[h]
            m_new = jnp.maximum(m_old, jnp.max(s, axis=-1, keepdims=True))
            alpha = jnp.exp2(m_old - m_new)
            p = jnp.exp2(s - jnp.concatenate([m_new] * (tq // dv), axis=1))
            l_ref[h] = l_ref[h] * alpha + jnp.sum(p, axis=-1, keepdims=True)
            acc_ref[h] = acc_ref[h] * alpha + _dot(p.astype(BF16), v)
            m_ref[h] = m_new

    def body(j, carry):
        step(j, False)
        return carry

    lax.fori_loop(0, i, body, 0)
    step(i, True)
    lam = _diff_lambda(lam_ref, lam_init)
    for h in range(DIFF_HEADS):
        o = acc_ref[h] / l_ref[h]
        a = o[:tq] - lam * o[tq:]
        o_ref[:, h * dv:(h + 1) * dv] = _rms(a, g_ref[...]) * (1.0 - lam_init)


def _diff_prompt(proj, lam_p, g_sub, *, n_seq, seq, tq, lam_init):
    nq = seq // tq
    dv = 2 * DIFF_DQK
    w = DIFF_HEADS * dv
    stat = pltpu.VMEM((DIFF_HEADS, 2 * tq, dv), F32)
    return pl.pallas_call(
        functools.partial(_diff_prompt_kernel, tq=tq, lam_init=lam_init),
        grid=(n_seq, nq),
        in_specs=[
            pl.BlockSpec((4, DIFF_DQK), lambda b, i: (0, 0)),
            pl.BlockSpec((tq, w), lambda b, i: (b * nq + i, COL_DQ * LANES // w)),
            pl.BlockSpec((seq, w), lambda b, i: (b, COL_DK * LANES // w)),
            pl.BlockSpec((seq, w), lambda b, i: (b, COL_DV * LANES // w)),
            pl.BlockSpec((1, dv), lambda b, i: (0, 0)),
        ],
        out_specs=pl.BlockSpec((tq, w), lambda b, i: (b * nq + i, 0)),
        out_shape=jax.ShapeDtypeStruct((n_seq * seq, w), F32),
        scratch_shapes=[pltpu.VMEM((seq, w), BF16), pltpu.VMEM((seq, w), BF16),
                        pltpu.VMEM((DIFF_HEADS, 2 * tq, dv), BF16), stat, stat, stat],
        compiler_params=_params("arbitrary", "arbitrary"),
        name="diff_attn_prompt",
    )(lam_p, proj, proj, proj, g_sub.reshape(1, dv))


def _diff_paged_kernel(pt_ref, lam_ref, q_ref, ks_ref, vs_ref, g_ref, *rest, t_new, n_pg, group, lam_init):
    del pt_ref
    k_pages = rest[:n_pg]
    v_pages = rest[n_pg:2 * n_pg]
    o_ref, qall_ref, m_ref, l_ref, acc_ref = rest[2 * n_pg:]
    T = t_new
    R = 2 * T
    HR = DIFF_HEADS * R
    dv = 2 * DIFF_DQK
    prow = k_pages[0].shape[0]
    step_idx = pl.program_id(1)

    @pl.when(step_idx == 0)
    def _():
        q = q_ref[...] * (DIFF_DQK ** -0.5)
        t_of_row = lax.broadcasted_iota(jnp.int32, (R, 1), 0) % T
        for h in range(DIFF_HEADS):
            sl = slice(h * dv, (h + 1) * dv)
            rows = slice(h * R, (h + 1) * R)
            qbd = _split_components(q[:, sl])
            qall_ref[rows, :] = qbd.astype(BF16)
            ks = ks_ref[:, sl]
            vs = vs_ref[:, sl]
            scores = []
            for j in range(T):
                sj = jnp.sum(qbd * ks[j:j + 1], axis=-1, keepdims=True)
                scores.append(jnp.where(t_of_row >= j, sj, -jnp.inf))
            m = scores[0]
            for j in range(1, T):
                m = jnp.maximum(m, scores[j])
            den = jnp.zeros((R, 1), F32)
            acc = jnp.zeros((R, dv), F32)
            for j in range(T):
                pj = jnp.exp(scores[j] - m)
                den = den + pj
                acc = acc + pj * vs[j:j + 1]
            m_ref[rows, :] = jnp.broadcast_to(m, (R, dv))
            l_ref[rows, :] = jnp.broadcast_to(den, (R, dv))
            acc_ref[rows, :] = acc

    qall = qall_ref[...]
    row_head = lax.broadcasted_iota(jnp.int32, (HR, prow), 0) // R
    col_head = lax.broadcasted_iota(jnp.int32, (HR, prow), 1) % DIFF_HEADS
    bias = jnp.where(row_head == col_head, 0.0, -jnp.inf)

    parts = []
    for g0 in range(0, n_pg, group):
        tiles = [_dot_nt(qall, k_pages[n][...].astype(BF16)) + bias for n in range(g0, g0 + group)]
        mx = tiles[0]
        for t in tiles[1:]:
            mx = jnp.maximum(mx, t)
        m_g = jnp.max(mx, axis=-1, keepdims=True)
        l_g = None
        acc_g = None
        for n, t in zip(range(g0, g0 + group), tiles):
            p = jnp.exp(t - m_g)
            ps = jnp.sum(p, axis=-1, keepdims=True)
            pv = _dot(p.astype(BF16), v_pages[n][...].astype(BF16))
            l_g = ps if l_g is None else l_g + ps
            acc_g = pv if acc_g is None else acc_g + pv
        parts.append((m_g, l_g, acc_g))

    m_old = m_ref[...]
    m_new = m_old
    for m_g, _, _ in parts:
        m_new = jnp.maximum(m_new, m_g)
    a_old = jnp.exp(m_old - m_new)
    l_new = l_ref[...] * a_old
    acc_new = acc_ref[...] * a_old
    for m_g, l_g, acc_g in parts:
        wgt = jnp.exp(m_g - m_new)
        l_new = l_new + l_g * wgt
        acc_new = acc_new + acc_g * wgt
    m_ref[...] = m_new
    l_ref[...] = l_new
    acc_ref[...] = acc_new

    @pl.when(step_idx == pl.num_programs(1) - 1)
    def _():
        lam = _diff_lambda(lam_ref, lam_init)
        o = acc_ref[...] / l_ref[...]
        for h in range(DIFF_HEADS):
            a = o[h * R:h * R + T] - lam * o[h * R + T:(h + 1) * R]
            o_ref[:, h * dv:(h + 1) * dv] = _rms(a, g_ref[...]) * (1.0 - lam_init)


def _diff_paged(proj, lam_p, g_sub, cache_k, cache_v, page_table, *, layer, t_new, lam_init):
    depth, n_pool, page, heads, kw = cache_k.shape
    n_seq, n_pages = page_table.shape
    assert heads == DIFF_HEADS and kw == 2 * DIFF_DQK and cache_v.shape == cache_k.shape
    n_pg = min(PAGES_PER_STEP, n_pages)
    group = min(PAGE_GROUP, n_pg)
    assert n_pages % n_pg == 0 and n_pg % group == 0
    n_steps = n_pages // n_pg
    dv = 2 * DIFF_DQK
    w = heads * kw
    ck = cache_k.reshape(depth * n_pool, page * heads, kw)
    cv = cache_v.reshape(depth * n_pool, page * heads, kw)
    pt = page_table.reshape(-1)
    base = layer * n_pool

    def page_spec(n):
        return pl.BlockSpec((None, page * heads, kw),
                            lambda b, s, pt: (base + pt[b * n_pages + s * n_pg + n], 0, 0))

    stat = pltpu.VMEM((DIFF_HEADS * 2 * t_new, dv), F32)
    grid_spec = pltpu.PrefetchScalarGridSpec(
        num_scalar_prefetch=1,
        grid=(n_seq, n_steps),
        in_specs=[
            pl.BlockSpec((4, DIFF_DQK), lambda b, s, pt: (0, 0)),
            pl.BlockSpec((t_new, w), lambda b, s, pt: (b, COL_DQ * LANES // w)),
            pl.BlockSpec((t_new, w), lambda b, s, pt: (b, COL_DK * LANES // w)),
            pl.BlockSpec((t_new, w), lambda b, s, pt: (b, COL_DV * LANES // w)),
            pl.BlockSpec((1, dv), lambda b, s, pt: (0, 0)),
        ] + [page_spec(n) for n in range(n_pg)] + [page_spec(n) for n in range(n_pg)],
        out_specs=pl.BlockSpec((t_new, w), lambda b, s, pt: (b, 0)),
        scratch_shapes=[pltpu.VMEM((DIFF_HEADS * 2 * t_new, dv), BF16), stat, stat, stat],
    )
    return pl.pallas_call(
        functools.partial(_diff_paged_kernel, t_new=t_new, n_pg=n_pg, group=group, lam_init=lam_init),
        grid_spec=grid_spec,
        out_shape=jax.ShapeDtypeStruct((n_seq * t_new, w), F32),
        compiler_params=_params("parallel", "arbitrary"),
        name="diff_attn_paged",
    )(pt, lam_p, proj, proj, proj, g_sub.reshape(1, dv), *([ck] * n_pg), *([cv] * n_pg))


def _cross_kernel(q_ref, mk_ref, mv_ref, o_ref):
    q = q_ref[...] * (XA_DH ** -0.5)
    for h in range(XA_HEADS):
        sl = slice(h * XA_DH, (h + 1) * XA_DH)
        s = _dot_nt(q[:, sl].astype(BF16), mk_ref[:, sl].astype(BF16))
        p = jnp.exp(s - jnp.max(s, axis=-1, keepdims=True))
        den = jnp.sum(p, axis=-1, keepdims=True)
        o_ref[:, sl] = _dot(p.astype(BF16), mv_ref[:, sl].astype(BF16)) / den


def _cross_attn(proj, mk, mv, mk_col, mv_col, *, n_seq, seq, n_mem, tq):
    nq = seq // tq
    w = XA_HEADS * XA_DH
    return pl.pallas_call(
        _cross_kernel,
        grid=(n_seq, nq),
        in_specs=[
            pl.BlockSpec((tq, w), lambda b, i: (b * nq + i, COL_XQ * LANES // w)),
            pl.BlockSpec((n_mem, w), lambda b, i: (b, mk_col)),
            pl.BlockSpec((n_mem, w), lambda b, i: (b, mv_col)),
        ],
        out_specs=pl.BlockSpec((tq, w), lambda b, i: (b * nq + i, 0)),
        out_shape=jax.ShapeDtypeStruct((n_seq * seq, w), F32),
        compiler_params=_params("parallel", "arbitrary"),
        name="cross_attn",
    )(proj, mk, mv)


def _merge_kernel(x_ref, oa_ref, ob_ref, oc_ref, ga_ref, gb_ref, gc_ref, wb_ref, wo_ref, g_ref, o_ref):
    merged = None
    for n, (br, gl) in enumerate(((oa_ref, ga_ref), (ob_ref, gb_ref), (oc_ref, gc_ref))):
        up = _dot(br[...].astype(BF16), wb_ref[n])
        term = _sigmoid(gl[...]) * up
        merged = term if merged is None else merged + term
    y = _dot(merged.astype(BF16), wo_ref[...])
    o_ref[...] = x_ref[...] + _rms(y, g_ref[...])


def _merge(x, oa, ob, oc, proj, w_branch, w_out, g_post, *, tm):
    m, d = x.shape
    gate0 = COL_GATE * LANES // d
    row = lambda i: (i, 0)
    return pl.pallas_call(
        _merge_kernel,
        grid=(m // tm,),
        in_specs=[
            pl.BlockSpec((tm, d), row),
            pl.BlockSpec((tm, BRANCH_W), row), pl.BlockSpec((tm, BRANCH_W), row), pl.BlockSpec((tm, BRANCH_W), row),
            pl.BlockSpec((tm, d), lambda i: (i, gate0)),
            pl.BlockSpec((tm, d), lambda i: (i, gate0 + 1)),
            pl.BlockSpec((tm, d), lambda i: (i, gate0 + 2)),
            pl.BlockSpec((N_BRANCH, BRANCH_W, d), lambda i: (0, 0, 0)),
            pl.BlockSpec((d, d), lambda i: (0, 0)),
            pl.BlockSpec((1, d), lambda i: (0, 0)),
        ],
        out_specs=pl.BlockSpec((tm, d), row),
        out_shape=jax.ShapeDtypeStruct((m, d), F32),
        compiler_params=_params("parallel"),
        name="merge",
    )(x, oa, ob, oc, proj, proj, proj, w_branch, w_out, g_post.reshape(1, d))


def _mlp_kernel(x_ref, gpre_ref, w1_ref, w2_ref, gpost_ref, o_ref, h_ref, acc_ref):
    j = pl.program_id(1)

    @pl.when(j == 0)
    def _():
        h_ref[...] = _rms(x_ref[...], gpre_ref[...]).astype(BF16)
        acc_ref[...] = jnp.zeros(acc_ref.shape, F32)

    a = jnp.square(jnp.maximum(_dot(h_ref[...], w1_ref[...]), 0.0))
    acc_ref[...] += _dot(a.astype(BF16), w2_ref[...])

    @pl.when(j == pl.num_programs(1) - 1)
    def _():
        o_ref[...] = x_ref[...] + _rms(acc_ref[...], gpost_ref[...])


def _mlp(x, g_pre, w1, w2, g_post, *, tm, tf):
    m, d = x.shape
    dff = w1.shape[1]
    return pl.pallas_call(
        _mlp_kernel,
        grid=(m // tm, dff // tf),
        in_specs=[
            pl.BlockSpec((tm, d), lambda i, j: (i, 0)),
            pl.BlockSpec((1, d), lambda i, j: (0, 0)),
            pl.BlockSpec((d, tf), lambda i, j: (0, j)),
            pl.BlockSpec((tf, d), lambda i, j: (j, 0)),
            pl.BlockSpec((1, d), lambda i, j: (0, 0)),
        ],
        out_specs=pl.BlockSpec((tm, d), lambda i, j: (i, 0)),
        out_shape=jax.ShapeDtypeStruct((m, d), F32),
        scratch_shapes=[pltpu.VMEM((tm, d), BF16), pltpu.VMEM((tm, d), F32)],
        compiler_params=_params("parallel", "arbitrary"),
        name="mlp",
    )(x, g_pre.reshape(1, d), w1, w2, g_post.reshape(1, d))


def _row_tile(m, cap):
    t = min(m, cap)
    assert m % t == 0
    return t


def _trunk_layer(x, w, lam_init, *, n_seq, seq, buf8, s0, mem, attend):
    m, d = x.shape
    tm = _row_tile(m, 1024)
    proj, ab = _norm_matmul(x, w["g_pre_mix"], w["w_main"], w["w_ab"], tm=tm, tn=1024)

    C = GDN_CHUNK
    seq_pad = -(-seq // C) * C
    if seq_pad == seq:
        xz, ab_p = proj, ab
    else:
        xz = jnp.pad(proj[:, :(COL_Z + 4) * LANES].reshape(n_seq, seq, -1), ((0, 0), (0, seq_pad - seq), (0, 0)))
        xz = xz.reshape(n_seq * seq_pad, -1)
        ab_p = jnp.pad(ab.reshape(n_seq, seq, -1), ((0, 0), (0, seq_pad - seq), (0, 0))).reshape(n_seq * seq_pad, -1)
    o_a, s_new = _gdn(xz, ab_p, buf8, s0, w["w_conv"], w["gdn_a_log"], w["gdn_dt_bias"], w["g_gdn_out"],
                      n_seq=n_seq, seq_rows=seq_pad, valid_len=seq)
    if seq_pad != seq:
        o_a = o_a.reshape(n_seq, seq_pad, -1)[:, :seq].reshape(m, -1)

    o_b = attend(proj)
    mk, mv, mk_col, mv_col, n_mem = mem
    o_c = _cross_attn(proj, mk, mv, mk_col, mv_col, n_seq=n_seq, seq=seq, n_mem=n_mem, tq=_row_tile(seq, 512))

    tmm = _row_tile(m, 512)
    x = _merge(x, o_a, o_b, o_c, proj, w["w_branch"], w["w_out"], w["g_post_mix"], tm=tmm)
    x = _mlp(x, w["g_pre_mlp"], w["w_ff1"], w["w_ff2"], w["g_post_mlp"], tm=tmm, tf=1024)
    return x, proj, s_new


def kernel(x_prompt, x_sample, mem_prompt, cache_diff_k, cache_diff_v, page_table, state_gdn, cache_gdn_conv,
           cache_mem_k, cache_mem_v, g_pre_mix, w_in, w_conv, gdn_a_log, gdn_dt_bias, g_gdn_out, diff_lambda,
           g_diff_sub, g_mem, w_mem_k, w_mem_v, w_branch, w_out, g_post_mix, g_pre_mlp, w_ff1, w_ff2, g_post_mlp):
    bp, sp, d = x_prompt.shape
    bs, ts, _ = x_sample.shape
    depth = w_in.shape[0]
    n_mem = mem_prompt.shape[1]
    qkv_w = 3 * GDN_HEADS * GDN_DK
    assert sp >= GDN_CONV - 1 and ts >= GDN_CONV - 1
    xa_w = XA_HEADS * XA_DH
    dk_w = DIFF_HEADS * 2 * DIFF_DQK

    xp = x_prompt.reshape(bp * sp, d)
    xs = x_sample.reshape(bs * ts, d)
    memf = mem_prompt.reshape(bp * n_mem, d)

    c_z = qkv_w
    c_a = c_z + BRANCH_W
    c_b = c_a + GDN_HEADS
    c_dq = c_b + GDN_HEADS

    outs = {k: [] for k in ("kp", "vp", "sp", "cp", "mk", "mv", "ks", "vs", "ss", "cs")}
    zero_buf = jnp.zeros((bp, SUBLANES, qkv_w), F32)
    zero_state = jnp.zeros((bp, GDN_HEADS, GDN_DK, GDN_DK), F32)
    for l in range(depth):
        lam_init = 0.8 - 0.6 * math.exp(-0.3 * l)
        w = {
            "g_pre_mix": g_pre_mix[l],
            "w_main": jnp.concatenate([w_in[l][:, :c_a], w_in[l][:, c_dq:]], axis=1).astype(BF16),
            "w_ab": w_in[l][:, c_a:c_dq].astype(BF16),
            "w_conv": w_conv[l], "gdn_a_log": gdn_a_log[l], "gdn_dt_bias": gdn_dt_bias[l],
            "g_gdn_out": g_gdn_out[l],
            "w_branch": w_branch[l].astype(BF16), "w_out": w_out[l].astype(BF16), "g_post_mix": g_post_mix[l],
            "g_pre_mlp": g_pre_mlp[l], "w_ff1": w_ff1[l].astype(BF16), "w_ff2": w_ff2[l].astype(BF16),
            "g_post_mlp": g_post_mlp[l],
        }
        lam_p = diff_lambda[l]
        g_sub = g_diff_sub[l]

        w_mem = jnp.concatenate([w_mem_k[l], w_mem_v[l]], axis=1).astype(BF16)
        memkv = _norm_matmul(memf, g_mem[l], w_mem, tm=_row_tile(bp * n_mem, 1024), tn=xa_w)
        attend_p = functools.partial(_diff_prompt, lam_p=lam_p, g_sub=g_sub, n_seq=bp, seq=sp,
                                     tq=_row_tile(sp, 256), lam_init=lam_init)
        xp, proj_p, s_p = _trunk_layer(xp, w, lam_init, n_seq=bp, seq=sp, buf8=zero_buf, s0=zero_state,
                                       mem=(memkv, memkv, 0, 1, n_mem), attend=attend_p)
        proj3 = proj_p.reshape(bp, sp, -1)
        outs["kp"].append(proj3[:, :, COL_DK * LANES:COL_DK * LANES + dk_w].reshape(bp, sp, DIFF_HEADS, -1))
        outs["vp"].append(proj3[:, :, COL_DV * LANES:COL_DV * LANES + dk_w].reshape(bp, sp, DIFF_HEADS, -1))
        outs["sp"].append(s_p)
        outs["cp"].append(proj3[:, sp - (GDN_CONV - 1):, :qkv_w])
        outs["mk"].append(memkv[:, :xa_w].reshape(bp, n_mem, XA_HEADS, XA_DH))
        outs["mv"].append(memkv[:, xa_w:].reshape(bp, n_mem, XA_HEADS, XA_DH))

        buf8 = jnp.pad(cache_gdn_conv[l], ((0, 0), (SUBLANES - (GDN_CONV - 1), 0), (0, 0)))
        attend_s = functools.partial(_diff_paged, lam_p=lam_p, g_sub=g_sub, cache_k=cache_diff_k,
                                     cache_v=cache_diff_v, page_table=page_table, layer=l, t_new=ts,
                                     lam_init=lam_init)
        mk_s = cache_mem_k[l].reshape(bs * n_mem, xa_w)
        mv_s = cache_mem_v[l].reshape(bs * n_mem, xa_w)
        xs, proj_s, s_s = _trunk_layer(xs, w, lam_init, n_seq=bs, seq=ts, buf8=buf8, s0=state_gdn[l],
                                       mem=(mk_s, mv_s, 0, 0, n_mem), attend=attend_s)
        proj3 = proj_s.reshape(bs, ts, -1)
        outs["ks"].append(proj3[:, :, COL_DK * LANES:COL_DK * LANES + dk_w].reshape(bs, ts, DIFF_HEADS, -1))
        outs["vs"].append(proj3[:, :, COL_DV * LANES:COL_DV * LANES + dk_w].reshape(bs, ts, DIFF_HEADS, -1))
        outs["ss"].append(s_s)
        outs["cs"].append(proj3[:, ts - (GDN_CONV - 1):, :qkv_w])

    st = lambda k: jnp.stack(outs[k])
    return (xp.reshape(bp, sp, d), xs.reshape(bs, ts, d),
            st("kp"), st("vp"), st("sp"), st("cp"), st("mk"), st("mv"),
            st("ks"), st("vs"), st("ss"), st("cs"))
```

```python
import functools
import math

import jax
import jax.numpy as jnp
from jax import lax
from jax.experimental import pallas as pl
from jax.experimental.pallas import tpu as pltpu

F32 = jnp.float32
BF16 = jnp.bfloat16
RMS_EPS = 1e-6
L2_EPS = 1e-6
LANES = 128
SUBLANES = 8
VMEM_LIMIT_BYTES = 48 * 1024 * 1024

GDN_HEADS = 4
GDN_DK = 128
GDN_CONV = 4
GDN_CHUNK = 128
DIFF_HEADS = 4
DIFF_DQK = 64
XA_HEADS = 4
XA_DH = 128
N_BRANCH = 3
BRANCH_W = 512
PAGES_PER_STEP = 16
PAGE_GROUP = 4
LOG2_E = math.log2(math.e)

COL_QKV, COL_Z, COL_DQ, COL_XQ, COL_GATE = 0, 12, 16, 20, 24
MAIN_COLS = 48 * LANES


def _params(*sem):
    return pltpu.CompilerParams(dimension_semantics=sem, vmem_limit_bytes=VMEM_LIMIT_BYTES)


def _rms(x, g):
    return x * lax.rsqrt(jnp.mean(x * x, axis=-1, keepdims=True) + RMS_EPS) * g


def _sigmoid(x):
    return 1.0 / (1.0 + jnp.exp(-x))


def _dot(a, b, precision=None):
    return jnp.dot(a, b, preferred_element_type=F32, precision=precision)


def _dot_nt(a, b, precision=None):
    return lax.dot_general(a, b, (((1,), (1,)), ((), ())), preferred_element_type=F32, precision=precision)


def _dot_tn(a, b, precision=None):
    return lax.dot_general(a, b, (((0,), (0,)), ((), ())), preferred_element_type=F32, precision=precision)


def _unit_lower_inverses(lmats, eye, level):
    C = eye.shape[0]
    ts = [eye - jnp.where(level <= 1, lm, 0.0) for lm in lmats]
    for k in range(2, C.bit_length()):
        t16s = [t.astype(BF16) for t in ts]
        xs = [_dot(jnp.where(level == k, lm, 0.0).astype(BF16), t16) for lm, t16 in zip(lmats, t16s)]
        ts = [t - _dot(t16, x.astype(BF16)) for t, t16, x in zip(ts, t16s, xs)]
    return ts


def _diff_lambda(lam_ref, lam_init):
    lp = lam_ref[...]
    a = jnp.sum(lp[0:1] * lp[1:2], axis=-1, keepdims=True)
    b = jnp.sum(lp[2:3] * lp[3:4], axis=-1, keepdims=True)
    return jnp.exp(a) - jnp.exp(b) + lam_init


def _norm_matmul_kernel(x_ref, g_ref, w_ref, o_ref, h_ref):
    @pl.when(pl.program_id(1) == 0)
    def _():
        h_ref[...] = _rms(x_ref[...], g_ref[...]).astype(BF16)

    o_ref[...] = _dot(h_ref[...], w_ref[...])


def _norm_matmul(x, g, w, *, tm, tn):
    m, d = x.shape
    n = w.shape[1]
    return pl.pallas_call(
        _norm_matmul_kernel,
        grid=(m // tm, n // tn),
        in_specs=[pl.BlockSpec((tm, d), lambda i, j: (i, 0)),
                  pl.BlockSpec((1, d), lambda i, j: (0, 0)),
                  pl.BlockSpec((d, tn), lambda i, j: (0, j))],
        out_specs=pl.BlockSpec((tm, tn), lambda i, j: (i, j)),
        out_shape=jax.ShapeDtypeStruct((m, n), F32),
        scratch_shapes=[pltpu.VMEM((tm, d), BF16)],
        compiler_params=_params("parallel", "arbitrary"),
        name="norm_matmul",
    )(x, g.reshape(1, d), w)


def _in_proj_kernel(x_ref, g_ref, w_ref, ws_ref, o_ref, os_ref, k_ref, v_ref, h_ref, *, n_main):
    j = pl.program_id(1)
    tm = x_ref.shape[0]
    kw = 2 * DIFF_DQK

    @pl.when(j == 0)
    def _():
        h = _rms(x_ref[...], g_ref[...]).astype(BF16)
        h_ref[...] = h
        os_ref[...] = _dot(h, ws_ref[...])

    res = _dot(h_ref[...], w_ref[...])

    @pl.when(j < n_main)
    def _():
        o_ref[...] = res

    @pl.when(j == n_main)
    def _():
        for hd in range(DIFF_HEADS):
            k_ref[pl.ds(hd, tm, stride=DIFF_HEADS), :] = res[:, hd * kw:(hd + 1) * kw]
            v_ref[pl.ds(hd, tm, stride=DIFF_HEADS), :] = res[:, (DIFF_HEADS + hd) * kw:(DIFF_HEADS + hd + 1) * kw]


def _in_proj(x, g, w, w_side, *, tm):
    m, d = x.shape
    tn = 2 * DIFF_HEADS * 2 * DIFF_DQK
    n_main = MAIN_COLS // tn
    assert w.shape[1] == MAIN_COLS + tn and MAIN_COLS % tn == 0
    ns = w_side.shape[1]
    kw = 2 * DIFF_DQK
    return pl.pallas_call(
        functools.partial(_in_proj_kernel, n_main=n_main),
        grid=(m // tm, n_main + 1),
        in_specs=[pl.BlockSpec((tm, d), lambda i, j: (i, 0)),
                  pl.BlockSpec((1, d), lambda i, j: (0, 0)),
                  pl.BlockSpec((d, tn), lambda i, j: (0, j)),
                  pl.BlockSpec((d, ns), lambda i, j: (0, 0))],
        out_specs=[pl.BlockSpec((tm, tn), lambda i, j: (i, jnp.minimum(j, n_main - 1))),
                   pl.BlockSpec((tm, ns), lambda i, j: (i, 0)),
                   pl.BlockSpec((tm * DIFF_HEADS, kw), lambda i, j: (i, 0)),
                   pl.BlockSpec((tm * DIFF_HEADS, kw), lambda i, j: (i, 0))],
        out_shape=[jax.ShapeDtypeStruct((m, MAIN_COLS), F32),
                   jax.ShapeDtypeStruct((m, ns), F32),
                   jax.ShapeDtypeStruct((m * DIFF_HEADS, kw), F32),
                   jax.ShapeDtypeStruct((m * DIFF_HEADS, kw), F32)],
        scratch_shapes=[pltpu.VMEM((tm, d), BF16)],
        compiler_params=_params("parallel", "arbitrary"),
        name="in_proj",
    )(x, g.reshape(1, d), w, w_side)


def _gdn_kernel(qkv_ref, ab_ref, z_ref, buf_ref, s0_ref, wconv_ref, alog_ref, dtb_ref, gout_ref,
                o_ref, sfin_ref, ext_ref, s_ref, prep_ref, *, chunk, valid_len):
    C = chunk
    H = GDN_HEADS
    DK = GDN_DK
    c = pl.program_id(1)
    hi = lax.Precision.HIGHEST

    @pl.when(c == 0)
    def _():
        ext_ref[0:SUBLANES, :] = buf_ref[...]
        s_ref[...] = s0_ref[...]

    row = lax.broadcasted_iota(jnp.int32, (C, C), 0)
    col = lax.broadcasted_iota(jnp.int32, (C, C), 1)
    incl = row >= col
    strict = row > col

    def prepare_block(blk):
        first = SUBLANES - (GDN_CONV - 1)
        sl = slice(blk * DK, (blk + 1) * DK)
        yb = ext_ref[first:first + C, sl] * wconv_ref[0:1, sl]
        for i in range(1, GDN_CONV):
            yb = yb + ext_ref[first + i:first + i + C, sl] * wconv_ref[i:i + 1, sl]
        yb = yb * _sigmoid(yb)
        if blk < 2 * H:
            yb = yb * lax.rsqrt(jnp.sum(yb * yb, axis=-1, keepdims=True) + L2_EPS)
        if blk < H:
            yb = yb * (DK ** -0.5)
        prep_ref[:, sl] = yb

    def prepare_gates():
        ab = ab_ref[...]
        sp_in = ab + dtb_ref[...]
        softplus = jnp.maximum(sp_in, 0.0) + jnp.log(1.0 + jnp.exp(-jnp.abs(sp_in)))
        g_all = -jnp.exp(alog_ref[...]) * softplus
        beta_all = _sigmoid(ab)
        if valid_len is not None:
            row_ok = (c * C + lax.broadcasted_iota(jnp.int32, (C, 1), 0)) < valid_len
            g_all = jnp.where(row_ok, g_all, 0.0)
            beta_all = jnp.where(row_ok, beta_all, 0.0)
        tril = jnp.where(incl, 1.0, 0.0).astype(F32)
        return _dot(tril, g_all, hi), beta_all

    def recur(g_cum_all, beta_all):
        eye = jnp.where(row == col, 1.0, 0.0).astype(F32)
        level = 32 - lax.clz(row ^ col)
        heads = range(H)
        q = [prep_ref[:, h * DK:(h + 1) * DK] for h in heads]
        k = [prep_ref[:, (H + h) * DK:(H + h + 1) * DK] for h in heads]
        v = [prep_ref[:, (2 * H + h) * DK:(2 * H + h + 1) * DK] for h in heads]
        beta = [beta_all[:, H + h:H + h + 1] for h in heads]
        g_cum = [g_cum_all[:, h:h + 1] for h in heads]
        decay = []
        for h in heads:
            g_cum_b = jnp.broadcast_to(g_cum[h], (C, C))
            decay.append(jnp.where(incl, jnp.exp(jnp.where(incl, g_cum_b - g_cum_b.T, 0.0)), 0.0))
        kb = [k[h] * beta[h] for h in heads]
        k16 = [x.astype(BF16) for x in k]
        lmat = [jnp.where(strict, _dot_nt(kb[h].astype(BF16), k16[h]) * decay[h], 0.0) for h in heads]
        qk =[(_dot_nt(q[h].astype(BF16), k16[h]) * decay[h]).astype(BF16) for h in heads]
        e_g = [jnp.exp(g_cum[h]) for h in heads]
        rhs = [jnp.concatenate([v[h] * beta[h], kb[h] * e_g[h]], axis=1).astype(BF16) for h in heads]
        g_last = [g_cum[h][C - 1:C] for h in heads]
        qg = [(q[h] * e_g[h]).astype(BF16) for h in heads]
        kg = [(k[h] * jnp.exp(g_last[h] - g_cum[h])).astype(BF16) for h in heads]
        tinv = _unit_lower_inverses(lmat, eye, level)
        sol = [_dot(tinv[h].astype(BF16), rhs[h]) for h in heads]
        s = [s_ref[h] for h in heads]
        s16 = [x.astype(BF16) for x in s]
        u = [sol[h][:, :DK] - _dot(sol[h][:, DK:].astype(BF16), s16[h]) for h in heads]
        u16 = [x.astype(BF16) for x in u]
        o = [_dot(qg[h], s16[h]) + _dot(qk[h], u16[h]) for h in heads]
        s_new = [s[h] * jnp.exp(g_last[h]) + _dot_tn(kg[h], u16[h]) for h in heads]
        for h in heads:
            s_ref[h] = s_new[h]
            zh = z_ref[:, h * DK:(h + 1) * DK]
            o_ref[:, h * DK:(h + 1) * DK] = _rms(o[h], gout_ref[...]) * (zh * _sigmoid(zh))

    ext_ref[SUBLANES:SUBLANES + C, :] = qkv_ref[...]
    for blk in range(3 * H):
        prepare_block(blk)
    recur(*prepare_gates())
    ext_ref[0:SUBLANES, :] = ext_ref[C:C + SUBLANES, :]

    @pl.when(c == pl.num_programs(1) - 1)
    def _():
        sfin_ref[...] = s_ref[...]


def _gdn(xz, ab, buf8, s0, w_conv, a_log, dt_bias, g_out, *, n_seq, seq_rows, valid_len):
    C = GDN_CHUNK
    nc = seq_rows // C
    qkv_w = 3 * GDN_HEADS * GDN_DK
    zeros_h = jnp.zeros((GDN_HEADS,), F32)
    alog8 = jnp.concatenate([a_log, zeros_h]).reshape(1, 2 * GDN_HEADS)
    dtb8 = jnp.concatenate([dt_bias, zeros_h]).reshape(1, 2 * GDN_HEADS)
    state_shape = (n_seq, GDN_HEADS, GDN_DK, GDN_DK)
    return pl.pallas_call(
        functools.partial(_gdn_kernel, chunk=C, valid_len=None if valid_len == seq_rows else valid_len),
        grid=(n_seq, nc),
        in_specs=[
            pl.BlockSpec((C, qkv_w), lambda b, c: (b * nc + c, 0)),
            pl.BlockSpec((C, 2 * GDN_HEADS), lambda b, c: (b * nc + c, 0)),
            pl.BlockSpec((C, BRANCH_W), lambda b, c: (b * nc + c, COL_Z * LANES // BRANCH_W)),
            pl.BlockSpec((None, SUBLANES, qkv_w), lambda b, c: (b, 0, 0)),
            pl.BlockSpec((None, GDN_HEADS, GDN_DK, GDN_DK), lambda b, c: (b, 0, 0, 0)),
            pl.BlockSpec((GDN_CONV, qkv_w), lambda b, c: (0, 0)),
            pl.BlockSpec((1, 2 * GDN_HEADS), lambda b, c: (0, 0)),
            pl.BlockSpec((1, 2 * GDN_HEADS), lambda b, c: (0, 0)),
            pl.BlockSpec((1, GDN_DK), lambda b, c: (0, 0)),
        ],
        out_specs=[
            pl.BlockSpec((C, BRANCH_W), lambda b, c: (b * nc + c, 0)),
            pl.BlockSpec((None, GDN_HEADS, GDN_DK, GDN_DK), lambda b, c: (b, 0, 0, 0)),
        ],
        out_shape=[jax.ShapeDtypeStruct((n_seq * seq_rows, BRANCH_W), F32),
                   jax.ShapeDtypeStruct(state_shape, F32)],
        scratch_shapes=[pltpu.VMEM((SUBLANES + C, qkv_w), F32),
                        pltpu.VMEM((GDN_HEADS, GDN_DK, GDN_DK), F32),
                        pltpu.VMEM((C, qkv_w), F32)],
        compiler_params=_params("parallel", "arbitrary"),
        name="gdn",
    )(xz, ab, xz, buf8, s0, w_conv, alog8, dtb8, g_out.reshape(1, GDN_DK))


def _split_components(q):
    lane = lax.broadcasted_iota(jnp.int32, q.shape, 1)
    return jnp.concatenate([jnp.where(lane < DIFF_DQK, q, 0.0), jnp.where(lane >= DIFF_DQK, q, 0.0)], axis=0)


def _diff_prompt_kernel(lam_ref, q_ref, k_ref, v_ref, g_ref, o_ref,
                        k16_ref, v16_ref, qbd_ref, m_ref, l_ref, acc_ref, *, tq, lam_init):
    i = pl.program_id(1)
    dv = 2 * DIFF_DQK

    @pl.when(i == 0)
    def _():
        seq = k16_ref.shape[1]
        for h in range(DIFF_HEADS):
            k16_ref[h] = k_ref[pl.ds(h, seq, stride=DIFF_HEADS), :].astype(BF16)
            v16_ref[h] = v_ref[pl.ds(h, seq, stride=DIFF_HEADS), :].astype(BF16)

    q = q_ref[...] * (DIFF_DQK ** -0.5 * LOG2_E)
    for h in range(DIFF_HEADS):
        qbd_ref[h] = _split_components(q[:, h * dv:(h + 1) * dv]).astype(BF16)
    m_ref[...] = jnp.full(m_ref.shape, -jnp.inf, F32)
    l_ref[...] = jnp.zeros(l_ref.shape, F32)
    acc_ref[...] = jnp.zeros(acc_ref.shape, F32)

    def step(j, masked):
        start = pl.multiple_of(j * tq, tq)
        for h in range(DIFF_HEADS):
            k = k16_ref[h, pl.ds(start, tq), :]
            v = v16_ref[h, pl.ds(start, tq), :]
            s = _dot_nt(qbd_ref[h], k)
            if masked:
                r = lax.broadcasted_iota(jnp.int32, (tq, tq), 0)
                cidx = lax.broadcasted_iota(jnp.int32, (tq, tq), 1)
                keep = jnp.concatenate([cidx <= r, cidx <= r], axis=0)
                s = jnp.where(keep, s, -jnp.inf)
            m_old = m_ref[h]
            m_new = jnp.maximum(m_old, jnp.max(s, axis=-1, keepdims=True))
            alpha = jnp.exp2(m_old - m_new)
            p = jnp.exp2(s - jnp.concatenate([m_new] * (tq // dv), axis=1))
            l_ref[h] = l_ref[h] * alpha + jnp.sum(p, axis=-1, keepdims=True)
            acc_ref[h] = acc_ref[h] * alpha + _dot(p.astype(BF16), v)
            m_ref[h] = m_new

    def body(j, carry):
        step(j, False)
        return carry

    lax.fori_loop(0, i, body, 0)
    step(i, True)
    lam = _diff_lambda(lam_ref, lam_init)
    for h in range(DIFF_HEADS):
        o = acc_ref[h] / l_ref[h]
        a = o[:tq] - lam * o[tq:]
        o_ref[:, h * dv:(h + 1) * dv] = _rms(a, g_ref[...]) * (1.0 - lam_init)


def _diff_prompt(proj, k_rows, v_rows, lam_p, g_sub, *, n_seq, seq, tq, lam_init):
    nq = seq // tq
    dv = 2 * DIFF_DQK
    w = DIFF_HEADS * dv
    stat = pltpu.VMEM((DIFF_HEADS, 2 * tq, dv), F32)
    kv16 = pltpu.VMEM((DIFF_HEADS, seq, dv), BF16)
    return pl.pallas_call(
        functools.partial(_diff_prompt_kernel, tq=tq, lam_init=lam_init),
        grid=(n_seq, nq),
        in_specs=[
            pl.BlockSpec((4, DIFF_DQK), lambda b, i: (0, 0)),
            pl.BlockSpec((tq, w), lambda b, i: (b * nq + i, COL_DQ * LANES // w)),
            pl.BlockSpec((seq * DIFF_HEADS, dv), lambda b, i: (b, 0)),
            pl.BlockSpec((seq * DIFF_HEADS, dv), lambda b, i: (b, 0)),
            pl.BlockSpec((1, dv), lambda b, i: (0, 0)),
        ],
        out_specs=pl.BlockSpec((tq, w), lambda b, i: (b * nq + i, 0)),
        out_shape=jax.ShapeDtypeStruct((n_seq * seq, w), F32),
        scratch_shapes=[kv16, kv16, pltpu.VMEM((DIFF_HEADS, 2 * tq, dv), BF16), stat, stat, stat],
        compiler_params=_params("arbitrary", "arbitrary"),
        name="diff_attn_prompt",
    )(lam_p, proj, k_rows, v_rows, g_sub.reshape(1, dv))


def _diff_paged_kernel(pt_ref, lam_ref, q_ref, ks_ref, vs_ref, g_ref, *rest, t_new, n_pg, group, lam_init):
    del pt_ref
    k_pages = rest[:n_pg]
    v_pages = rest[n_pg:2 * n_pg]
    o_ref, qall_ref, m_ref, l_ref, acc_ref = rest[2 * n_pg:]
    T = t_new
    R = 2 * T
    HR = DIFF_HEADS * R
    dv = 2 * DIFF_DQK
    prow = k_pages[0].shape[0]
    step_idx = pl.program_id(1)

    @pl.when(step_idx == 0)
    def _():
        q = q_ref[...] * (DIFF_DQK ** -0.5)
        t_of_row = lax.broadcasted_iota(jnp.int32, (R, 1), 0) % T
        for h in range(DIFF_HEADS):
            sl = slice(h * dv, (h + 1) * dv)
            rows = slice(h * R, (h + 1) * R)
            qbd = _split_components(q[:, sl])
            qall_ref[rows, :] = qbd.astype(BF16)
            ks = ks_ref[pl.ds(h, T, stride=DIFF_HEADS), :]
            vs = vs_ref[pl.ds(h, T, stride=DIFF_HEADS), :]
            scores = []
            for j in range(T):
                sj = jnp.sum(qbd * ks[j:j + 1], axis=-1, keepdims=True)
                scores.append(jnp.where(t_of_row >= j, sj, -jnp.inf))
            m = scores[0]
            for j in range(1, T):
                m = jnp.maximum(m, scores[j])
            den = jnp.zeros((R, 1), F32)
            acc = jnp.zeros((R, dv), F32)
            for j in range(T):
                pj = jnp.exp(scores[j] - m)
                den = den + pj
                acc = acc + pj * vs[j:j + 1]
            m_ref[rows, :] = jnp.broadcast_to(m, (R, dv))
            l_ref[rows, :] = jnp.broadcast_to(den, (R, dv))
            acc_ref[rows, :] = acc

    qall = qall_ref[...]
    row_head = lax.broadcasted_iota(jnp.int32, (HR, prow), 0) // R
    col_head = lax.broadcasted_iota(jnp.int32, (HR, prow), 1) % DIFF_HEADS
    bias = jnp.where(row_head == col_head, 0.0, -jnp.inf)

    parts = []
    for g0 in range(0, n_pg, group):
        tiles = [_dot_nt(qall, k_pages[n][...].astype(BF16)) + bias for n in range(g0, g0 + group)]
        mx = tiles[0]
        for t in tiles[1:]:
            mx = jnp.maximum(mx, t)
        m_g = jnp.max(mx, axis=-1, keepdims=True)
        l_g = None
        acc_g = None
        for n, t in zip(range(g0, g0 + group), tiles):
            p = jnp.exp(t - m_g)
            ps = jnp.sum(p, axis=-1, keepdims=True)
            pv = _dot(p.astype(BF16), v_pages[n][...].astype(BF16))
            l_g = ps if l_g is None else l_g + ps
            acc_g = pv if acc_g is None else acc_g + pv
        parts.append((m_g, l_g, acc_g))

    m_old = m_ref[...]
    m_new = m_old
    for m_g, _, _ in parts:
        m_new = jnp.maximum(m_new, m_g)
    a_old = jnp.exp(m_old - m_new)
    l_new = l_ref[...] * a_old
    acc_new = acc_ref[...] * a_old
    for m_g, l_g, acc_g in parts:
        wgt = jnp.exp(m_g - m_new)
        l_new = l_new + l_g * wgt
        acc_new = acc_new + acc_g * wgt
    m_ref[...] = m_new
    l_ref[...] = l_new
    acc_ref[...] = acc_new

    @pl.when(step_idx == pl.num_programs(1) - 1)
    def _():
        lam = _diff_lambda(lam_ref, lam_init)
        o = acc_ref[...] / l_ref[...]
        for h in range(DIFF_HEADS):
            a = o[h * R:h * R + T] - lam * o[h * R + T:(h + 1) * R]
            o_ref[:, h * dv:(h + 1) * dv] = _rms(a, g_ref[...]) * (1.0 - lam_init)


def _diff_paged(proj, k_rows, v_rows, lam_p, g_sub, cache_k, cache_v, page_table, *, layer, t_new, lam_init):
    depth, n_pool, page, heads, kw = cache_k.shape
    n_seq, n_pages = page_table.shape
    assert heads == DIFF_HEADS and kw == 2 * DIFF_DQK and cache_v.shape == cache_k.shape
    n_pg = min(PAGES_PER_STEP, n_pages)
    group = min(PAGE_GROUP, n_pg)
    assert n_pages % n_pg == 0 and n_pg % group == 0
    n_steps = n_pages // n_pg
    dv = 2 * DIFF_DQK
    w = heads * kw
    ck = cache_k.reshape(depth * n_pool, page * heads, kw)
    cv = cache_v.reshape(depth * n_pool, page * heads, kw)
    pt = page_table.reshape(-1)
    base = layer * n_pool

    def page_spec(n):
        return pl.BlockSpec((None, page * heads, kw),
                            lambda b, s, pt: (base + pt[b * n_pages + s * n_pg + n], 0, 0))

    stat = pltpu.VMEM((DIFF_HEADS * 2 * t_new, dv), F32)
    grid_spec = pltpu.PrefetchScalarGridSpec(
        num_scalar_prefetch=1,
        grid=(n_seq, n_steps),
        in_specs=[
            pl.BlockSpec((4, DIFF_DQK), lambda b, s, pt: (0, 0)),
            pl.BlockSpec((t_new, w), lambda b, s, pt: (b, COL_DQ * LANES // w)),
            pl.BlockSpec((t_new * heads, kw), lambda b, s, pt: (b, 0)),
            pl.BlockSpec((t_new * heads, kw), lambda b, s, pt: (b, 0)),
            pl.BlockSpec((1, dv), lambda b, s, pt: (0, 0)),
        ] + [page_spec(n) for n in range(n_pg)] + [page_spec(n) for n in range(n_pg)],
        out_specs=pl.BlockSpec((t_new, w), lambda b, s, pt: (b, 0)),
        scratch_shapes=[pltpu.VMEM((DIFF_HEADS * 2 * t_new, dv), BF16), stat, stat, stat],
    )
    return pl.pallas_call(
        functools.partial(_diff_paged_kernel, t_new=t_new, n_pg=n_pg, group=group, lam_init=lam_init),
        grid_spec=grid_spec,
        out_shape=jax.ShapeDtypeStruct((n_seq * t_new, w), F32),
        compiler_params=_params("parallel", "arbitrary"),
        name="diff_attn_paged",
    )(pt, lam_p, proj, k_rows, v_rows, g_sub.reshape(1, dv), *([ck] * n_pg), *([cv] * n_pg))


def _cross_kernel(q_ref, mk_ref, mv_ref, o_ref):
    q = q_ref[...] * (XA_DH ** -0.5)
    for h in range(XA_HEADS):
        sl = slice(h * XA_DH, (h + 1) * XA_DH)
        s = _dot_nt(q[:, sl].astype(BF16), mk_ref[:, sl].astype(BF16))
        p = jnp.exp(s - jnp.max(s, axis=-1, keepdims=True))
        den = jnp.sum(p, axis=-1, keepdims=True)
        o_ref[:, sl] = _dot(p.astype(BF16), mv_ref[:, sl].astype(BF16)) / den


def _cross_attn(proj, mk, mv, mk_col, mv_col, *, n_seq, seq, n_mem, tq):
    nq = seq // tq
    w = XA_HEADS * XA_DH
    return pl.pallas_call(
        _cross_kernel,
        grid=(n_seq, nq),
        in_specs=[
            pl.BlockSpec((tq, w), lambda b, i: (b * nq + i, COL_XQ * LANES // w)),
            pl.BlockSpec((n_mem, w), lambda b, i: (b, mk_col)),
            pl.BlockSpec((n_mem, w), lambda b, i: (b, mv_col)),
        ],
        out_specs=pl.BlockSpec((tq, w), lambda b, i: (b * nq + i, 0)),
        out_shape=jax.ShapeDtypeStruct((n_seq * seq, w), F32),
        compiler_params=_params("parallel", "arbitrary"),
        name="cross_attn",
    )(proj, mk, mv)


def _merge_kernel(x_ref, oa_ref, ob_ref, oc_ref, ga_ref, gb_ref, gc_ref, wb_ref, wo_ref, g_ref, o_ref):
    merged = None
    for n, (br, gl) in enumerate(((oa_ref, ga_ref), (ob_ref, gb_ref), (oc_ref, gc_ref))):
        up = _dot(br[...].astype(BF16), wb_ref[n])
        term = _sigmoid(gl[...]) * up
        merged = term if merged is None else merged + term
    y = _dot(merged.astype(BF16), wo_ref[...])
    o_ref[...] = x_ref[...] + _rms(y, g_ref[...])


def _merge(x, oa, ob, oc, proj, w_branch, w_out, g_post, *, tm):
    m, d = x.shape
    gate0 = COL_GATE * LANES // d
    row = lambda i: (i, 0)
    return pl.pallas_call(
        _merge_kernel,
        grid=(m // tm,),
        in_specs=[
            pl.BlockSpec((tm, d), row),
            pl.BlockSpec((tm, BRANCH_W), row), pl.BlockSpec((tm, BRANCH_W), row), pl.BlockSpec((tm, BRANCH_W), row),
            pl.BlockSpec((tm, d), lambda i: (i, gate0)),
            pl.BlockSpec((tm, d), lambda i: (i, gate0 + 1)),
            pl.BlockSpec((tm, d), lambda i: (i, gate0 + 2)),
            pl.BlockSpec((N_BRANCH, BRANCH_W, d), lambda i: (0, 0, 0)),
            pl.BlockSpec((d, d), lambda i: (0, 0)),
            pl.BlockSpec((1, d), lambda i: (0, 0)),
        ],
        out_specs=pl.BlockSpec((tm, d), row),
        out_shape=jax.ShapeDtypeStruct((m, d), F32),
        compiler_params=_params("parallel"),
        name="merge",
    )(x, oa, ob, oc, proj, proj, proj, w_branch, w_out, g_post.reshape(1, d))


def _mlp_kernel(x_ref, gpre_ref, w1_ref, w2_ref, gpost_ref, o_ref, h_ref, acc_ref):
    j = pl.program_id(1)

    @pl.when(j == 0)
    def _():
        h_ref[...] = _rms(x_ref[...], gpre_ref[...]).astype(BF16)
        acc_ref[...] = jnp.zeros(acc_ref.shape, F32)

    a = jnp.square(jnp.maximum(_dot(h_ref[...], w1_ref[...]), 0.0))
    acc_ref[...] += _dot(a.astype(BF16), w2_ref[...])

    @pl.when(j == pl.num_programs(1) - 1)
    def _():
        o_ref[...] = x_ref[...] + _rms(acc_ref[...], gpost_ref[...])


def _mlp(x, g_pre, w1, w2, g_post, *, tm, tf):
    m, d = x.shape
    dff = w1.shape[1]
    return pl.pallas_call(
        _mlp_kernel,
        grid=(m // tm, dff // tf),
        in_specs=[
            pl.BlockSpec((tm, d), lambda i, j: (i, 0)),
            pl.BlockSpec((1, d), lambda i, j: (0, 0)),
            pl.BlockSpec((d, tf), lambda i, j: (0, j)),
            pl.BlockSpec((tf, d), lambda i, j: (j, 0)),
            pl.BlockSpec((1, d), lambda i, j: (0, 0)),
        ],
        out_specs=pl.BlockSpec((tm, d), lambda i, j: (i, 0)),
        out_shape=jax.ShapeDtypeStruct((m, d), F32),
        scratch_shapes=[pltpu.VMEM((tm, d), BF16), pltpu.VMEM((tm, d), F32)],
        compiler_params=_params("parallel", "arbitrary"),
        name="mlp",
    )(x, g_pre.reshape(1, d), w1, w2, g_post.reshape(1, d))


def _row_tile(m, cap):
    t = min(m, cap)
    assert m % t == 0
    return t


def _trunk_layer(x, w, lam_init, *, n_seq, seq, buf8, s0, mem, attend):
    m, d = x.shape
    proj, ab, k_rows, v_rows = _in_proj(x, w["g_pre_mix"], w["w_main"], w["w_ab"], tm=_row_tile(m, 1024))

    C = GDN_CHUNK
    seq_pad = -(-seq // C) * C
    if seq_pad == seq:
        xz, ab_p = proj, ab
    else:
        xz = jnp.pad(proj[:, :(COL_Z + 4) * LANES].reshape(n_seq, seq, -1), ((0, 0), (0, seq_pad - seq), (0, 0)))
        xz = xz.reshape(n_seq * seq_pad, -1)
        ab_p = jnp.pad(ab.reshape(n_seq, seq, -1), ((0, 0), (0, seq_pad - seq), (0, 0))).reshape(n_seq * seq_pad, -1)
    o_a, s_new = _gdn(xz, ab_p, buf8, s0, w["w_conv"], w["gdn_a_log"], w["gdn_dt_bias"], w["g_gdn_out"],
                      n_seq=n_seq, seq_rows=seq_pad, valid_len=seq)
    if seq_pad != seq:
        o_a = o_a.reshape(n_seq, seq_pad, -1)[:, :seq].reshape(m, -1)

    o_b = attend(proj, k_rows, v_rows)
    mk, mv, mk_col, mv_col, n_mem = mem
    o_c = _cross_attn(proj, mk, mv, mk_col, mv_col, n_seq=n_seq, seq=seq, n_mem=n_mem, tq=_row_tile(seq, 512))

    x = _merge(x, o_a, o_b, o_c, proj, w["w_branch"], w["w_out"], w["g_post_mix"], tm=_row_tile(m, 512))
    x = _mlp(x, w["g_pre_mlp"], w["w_ff1"], w["w_ff2"], w["g_post_mlp"], tm=_row_tile(m, 1024), tf=1024)
    return x, proj, k_rows, v_rows, s_new


def kernel(x_prompt, x_sample, mem_prompt, cache_diff_k, cache_diff_v, page_table, state_gdn, cache_gdn_conv,
           cache_mem_k, cache_mem_v, g_pre_mix, w_in, w_conv, gdn_a_log, gdn_dt_bias, g_gdn_out, diff_lambda,
           g_diff_sub, g_mem, w_mem_k, w_mem_v, w_branch, w_out, g_post_mix, g_pre_mlp, w_ff1, w_ff2, g_post_mlp):
    bp, sp, d = x_prompt.shape
    bs, ts, _ = x_sample.shape
    depth = w_in.shape[0]
    n_mem = mem_prompt.shape[1]
    qkv_w = 3 * GDN_HEADS * GDN_DK
    assert sp >= GDN_CONV - 1 and ts >= GDN_CONV - 1
    xa_w = XA_HEADS * XA_DH
    dk_w = DIFF_HEADS * 2 * DIFF_DQK

    xp = x_prompt.reshape(bp * sp, d)
    xs = x_sample.reshape(bs * ts, d)
    memf = mem_prompt.reshape(bp * n_mem, d)

    c_a = qkv_w + BRANCH_W
    c_dq = c_a + 2 * GDN_HEADS
    c_dk = c_dq + dk_w
    c_xq = c_dk + 2 * dk_w

    outs = {k: [] for k in ("kp", "vp", "sp", "cp", "mk", "mv", "ks", "vs", "ss", "cs")}
    zero_buf = jnp.zeros((bp, SUBLANES, qkv_w), F32)
    zero_state = jnp.zeros((bp, GDN_HEADS, GDN_DK, GDN_DK), F32)
    for l in range(depth):
        lam_init = 0.8 - 0.6 * math.exp(-0.3 * l)
        w = {
            "g_pre_mix": g_pre_mix[l],
            "w_main": jnp.concatenate([w_in[l][:, :c_a], w_in[l][:, c_dq:c_dk], w_in[l][:, c_xq:],
                                       w_in[l][:, c_dk:c_xq]], axis=1).astype(BF16),
            "w_ab": w_in[l][:, c_a:c_dq].astype(BF16),
            "w_conv": w_conv[l], "gdn_a_log": gdn_a_log[l], "gdn_dt_bias": gdn_dt_bias[l],
            "g_gdn_out": g_gdn_out[l],
            "w_branch": w_branch[l].astype(BF16), "w_out": w_out[l].astype(BF16), "g_post_mix": g_post_mix[l],
            "g_pre_mlp": g_pre_mlp[l], "w_ff1": w_ff1[l].astype(BF16), "w_ff2": w_ff2[l].astype(BF16),
            "g_post_mlp": g_post_mlp[l],
        }
        lam_p = diff_lambda[l]
        g_sub = g_diff_sub[l]

        w_mem = jnp.concatenate([w_mem_k[l], w_mem_v[l]], axis=1).astype(BF16)
        memkv = _norm_matmul(memf, g_mem[l], w_mem, tm=_row_tile(bp * n_mem, 1024), tn=xa_w)
        attend_p = functools.partial(_diff_prompt, lam_p=lam_p, g_sub=g_sub, n_seq=bp, seq=sp,
                                     tq=_row_tile(sp, 256), lam_init=lam_init)
        xp, proj_p, k_p, v_p, s_p = _trunk_layer(xp, w, lam_init, n_seq=bp, seq=sp, buf8=zero_buf, s0=zero_state,
                                                 mem=(memkv, memkv, 0, 1, n_mem), attend=attend_p)
        proj3 = proj_p.reshape(bp, sp, -1)
        outs["kp"].append(k_p.reshape(bp, sp, DIFF_HEADS, -1))
        outs["vp"].append(v_p.reshape(bp, sp, DIFF_HEADS, -1))
        outs["sp"].append(s_p)
        outs["cp"].append(proj3[:, sp - (GDN_CONV - 1):, :qkv_w])
        outs["mk"].append(memkv[:, :xa_w].reshape(bp, n_mem, XA_HEADS, XA_DH))
        outs["mv"].append(memkv[:, xa_w:].reshape(bp, n_mem, XA_HEADS, XA_DH))

        buf8 = jnp.pad(cache_gdn_conv[l], ((0, 0), (SUBLANES - (GDN_CONV - 1), 0), (0, 0)))
        attend_s = functools.partial(_diff_paged, lam_p=lam_p, g_sub=g_sub, cache_k=cache_diff_k,
                                     cache_v=cache_diff_v, page_table=page_table, layer=l, t_new=ts,
                                     lam_init=lam_init)
        mk_s = cache_mem_k[l].reshape(bs * n_mem, xa_w)
        mv_s = cache_mem_v[l].reshape(bs * n_mem, xa_w)
        xs, proj_s, k_s, v_s, s_s = _trunk_layer(xs, w, lam_init, n_seq=bs, seq=ts, buf8=buf8, s0=state_gdn[l],
                                                 mem=(mk_s, mv_s, 0, 0, n_mem), attend=attend_s)
        proj3 = proj_s.reshape(bs, ts, -1)
        outs["ks"].append(k_s.reshape(bs, ts, DIFF_HEADS, -1))
        outs["vs"].append(v_s.reshape(bs, ts, DIFF_HEADS, -1))
        outs["ss"].append(s_s)
        outs["cs"].append(proj3[:, ts - (GDN_CONV - 1):, :qkv_w])

    st = lambda k: jnp.stack(outs[k])
    return (xp.reshape(bp, sp, d), xs.reshape(bs, ts, d),
            st("kp"), st("vp"), st("sp"), st("cp"), st("mk"), st("mv"),
            st("ks"), st("vs"), st("ss"), st("cs"))
```

```python
import functools
import math

import jax
import jax.numpy as jnp
from jax import lax
from jax.experimental import pallas as pl
from jax.experimental.pallas import tpu as pltpu

F32 = jnp.float32
BF16 = jnp.bfloat16
RMS_EPS = 1e-6
L2_EPS = 1e-6
LANES = 128
SUBLANES = 8
VMEM_LIMIT_BYTES = 48 * 1024 * 1024

GDN_HEADS = 4
GDN_DK = 128
GDN_CONV = 4
GDN_CHUNK = 128
GDN_SEQS_PER_STEP = 2
DIFF_HEADS = 4
DIFF_DQK = 64
XA_HEADS = 4
XA_DH = 128
N_BRANCH = 3
BRANCH_W = 512
PAGES_PER_STEP = 32
PAGE_GROUP = 4
LOG2_E = math.log2(math.e)

COL_QKV, COL_Z, COL_DQ, COL_XQ, COL_GATE = 0, 12, 16, 20, 24
MAIN_COLS = 48 * LANES


def _params(*sem):
    return pltpu.CompilerParams(dimension_semantics=sem, vmem_limit_bytes=VMEM_LIMIT_BYTES)


def _rms(x, g):
    return x * lax.rsqrt(jnp.mean(x * x, axis=-1, keepdims=True) + RMS_EPS) * g


def _sigmoid(x):
    return 1.0 / (1.0 + jnp.exp(-x))


def _dot(a, b, precision=None):
    return jnp.dot(a, b, preferred_element_type=F32, precision=precision)


def _dot_nt(a, b, precision=None):
    return lax.dot_general(a, b, (((1,), (1,)), ((), ())), preferred_element_type=F32, precision=precision)


def _dot_tn(a, b, precision=None):
    return lax.dot_general(a, b, (((0,), (0,)), ((), ())), preferred_element_type=F32, precision=precision)


def _unit_lower_inverses(lmats, eye, level, max_level):
    ts = [eye - jnp.where(level <= 1, lm, 0.0) for lm in lmats]
    for k in range(2, max_level + 1):
        t16s = [t.astype(BF16) for t in ts]
        xs = [_dot(jnp.where(level == k, lm, 0.0).astype(BF16), t16) for lm, t16 in zip(lmats, t16s)]
        ts = [t - _dot(t16, x.astype(BF16)) for t, t16, x in zip(ts, t16s, xs)]
    return ts


def _diff_lambda(lam_ref, lam_init):
    lp = lam_ref[...]
    a = jnp.sum(lp[0:1] * lp[1:2], axis=-1, keepdims=True)
    b = jnp.sum(lp[2:3] * lp[3:4], axis=-1, keepdims=True)
    return jnp.exp(a) - jnp.exp(b) + lam_init


def _norm_matmul_kernel(x_ref, g_ref, w_ref, o_ref, h_ref):
    @pl.when(pl.program_id(1) == 0)
    def _():
        h_ref[...] = _rms(x_ref[...], g_ref[...]).astype(BF16)

    o_ref[...] = _dot(h_ref[...], w_ref[...])


def _norm_matmul(x, g, w, *, tm, tn):
    m, d = x.shape
    n = w.shape[1]
    return pl.pallas_call(
        _norm_matmul_kernel,
        grid=(m // tm, n // tn),
        in_specs=[pl.BlockSpec((tm, d), lambda i, j: (i, 0)),
                  pl.BlockSpec((1, d), lambda i, j: (0, 0)),
                  pl.BlockSpec((d, tn), lambda i, j: (0, j))],
        out_specs=pl.BlockSpec((tm, tn), lambda i, j: (i, j)),
        out_shape=jax.ShapeDtypeStruct((m, n), F32),
        scratch_shapes=[pltpu.VMEM((tm, d), BF16)],
        compiler_params=_params("parallel", "arbitrary"),
        name="norm_matmul",
    )(x, g.reshape(1, d), w)


def _in_proj_kernel(x_ref, g_ref, w_ref, ws_ref, o_ref, os_ref, k_ref, v_ref, h_ref, *, n_main):
    j = pl.program_id(1)
    tm = x_ref.shape[0]
    kw = 2 * DIFF_DQK

    @pl.when(j == 0)
    def _():
        h = _rms(x_ref[...], g_ref[...]).astype(BF16)
        h_ref[...] = h
        os_ref[...] = _dot(h, ws_ref[...])

    res = _dot(h_ref[...], w_ref[...])

    @pl.when(j < n_main)
    def _():
        o_ref[...] = res

    @pl.when(j == n_main)
    def _():
        for hd in range(DIFF_HEADS):
            k_ref[pl.ds(hd, tm, stride=DIFF_HEADS), :] = res[:, hd * kw:(hd + 1) * kw]
            v_ref[pl.ds(hd, tm, stride=DIFF_HEADS), :] = res[:, (DIFF_HEADS + hd) * kw:(DIFF_HEADS + hd + 1) * kw]


def _in_proj(x, g, w, w_side, *, tm):
    m, d = x.shape
    tn = 2 * DIFF_HEADS * 2 * DIFF_DQK
    n_main = MAIN_COLS // tn
    assert w.shape[1] == MAIN_COLS + tn and MAIN_COLS % tn == 0
    ns = w_side.shape[1]
    kw = 2 * DIFF_DQK
    return pl.pallas_call(
        functools.partial(_in_proj_kernel, n_main=n_main),
        grid=(m // tm, n_main + 1),
        in_specs=[pl.BlockSpec((tm, d), lambda i, j: (i, 0)),
                  pl.BlockSpec((1, d), lambda i, j: (0, 0)),
                  pl.BlockSpec((d, tn), lambda i, j: (0, j)),
                  pl.BlockSpec((d, ns), lambda i, j: (0, 0))],
        out_specs=[pl.BlockSpec((tm, tn), lambda i, j: (i, jnp.minimum(j, n_main - 1))),
                   pl.BlockSpec((tm, ns), lambda i, j: (i, 0)),
                   pl.BlockSpec((tm * DIFF_HEADS, kw), lambda i, j: (i, 0)),
                   pl.BlockSpec((tm * DIFF_HEADS, kw), lambda i, j: (i, 0))],
        out_shape=[jax.ShapeDtypeStruct((m, MAIN_COLS), F32),
                   jax.ShapeDtypeStruct((m, ns), F32),
                   jax.ShapeDtypeStruct((m * DIFF_HEADS, kw), F32),
                   jax.ShapeDtypeStruct((m * DIFF_HEADS, kw), F32)],
        scratch_shapes=[pltpu.VMEM((tm, d), BF16)],
        compiler_params=_params("parallel", "arbitrary"),
        name="in_proj",
    )(x, g.reshape(1, d), w, w_side)


def _gdn_kernel(qkv_ref, ab_ref, z_ref, buf_ref, s0_ref, wconv_ref, alog_ref, dtb_ref, gout_ref,
                o_ref, sfin_ref, ext_ref, s_ref, prep_ref, *, chunk, valid_len):
    C = chunk
    G = qkv_ref.shape[0]
    H = GDN_HEADS
    DK = GDN_DK
    c = pl.program_id(1)
    hi = lax.Precision.HIGHEST
    live = C if valid_len is None else min(C, -(-valid_len // SUBLANES) * SUBLANES)

    @pl.when(c == 0)
    def _():
        ext_ref[:, 0:SUBLANES, :] = buf_ref[...]
        s_ref[...] = s0_ref[...]

    row = lax.broadcasted_iota(jnp.int32, (C, C), 0)
    col = lax.broadcasted_iota(jnp.int32, (C, C), 1)
    incl = row >= col
    strict = row > col

    def prepare_block(g, blk):
        first = SUBLANES - (GDN_CONV - 1)
        sl = slice(blk * DK, (blk + 1) * DK)
        yb = ext_ref[g, first:first + live, sl] * wconv_ref[0:1, sl]
        for i in range(1, GDN_CONV):
            yb = yb + ext_ref[g, first + i:first + i + live, sl] * wconv_ref[i:i + 1, sl]
        yb = yb * _sigmoid(yb)
        if blk < 2 * H:
            yb = yb * lax.rsqrt(jnp.sum(yb * yb, axis=-1, keepdims=True) + L2_EPS)
        if blk < H:
            yb = yb * (DK ** -0.5)
        prep_ref[g, 0:live, sl] = yb
        if live < C:
            prep_ref[g, live:C, sl] = jnp.zeros((C - live, DK), F32)

    def prepare_gates(g):
        ab = ab_ref[g]
        sp_in = ab + dtb_ref[...]
        softplus = jnp.maximum(sp_in, 0.0) + jnp.log(1.0 + jnp.exp(-jnp.abs(sp_in)))
        g_all = -jnp.exp(alog_ref[...]) * softplus
        beta_all = _sigmoid(ab)
        if valid_len is not None:
            row_ok = (c * C + lax.broadcasted_iota(jnp.int32, (C, 1), 0)) < valid_len
            g_all = jnp.where(row_ok, g_all, 0.0)
            beta_all = jnp.where(row_ok, beta_all, 0.0)
        tril = jnp.where(incl, 1.0, 0.0).astype(F32)
        return _dot(tril, g_all, hi), beta_all

    def recur(gates):
        eye = jnp.where(row == col, 1.0, 0.0).astype(F32)
        level = 32 - lax.clz(row ^ col)
        pairs = [(g, h) for g in range(G) for h in range(H)]
        idx = range(len(pairs))
        q = [prep_ref[g, :, h * DK:(h + 1) * DK] for g, h in pairs]
        k = [prep_ref[g, :, (H + h) * DK:(H + h + 1) * DK] for g, h in pairs]
        v = [prep_ref[g, :, (2 * H + h) * DK:(2 * H + h + 1) * DK] for g, h in pairs]
        beta = [gates[g][1][:, H + h:H + h + 1] for g, h in pairs]
        g_cum = [gates[g][0][:, h:h + 1] for g, h in pairs]
        decay = []
        for p in idx:
            g_cum_b = jnp.broadcast_to(g_cum[p], (C, C))
            decay.append(jnp.where(incl, jnp.exp(jnp.where(incl, g_cum_b - g_cum_b.T, 0.0)), 0.0))
        kb = [k[p] * beta[p] for p in idx]
        k16 = [x.astype(BF16) for x in k]
        lmat = [jnp.where(strict, _dot_nt(kb[p].astype(BF16), k16[p]) * decay[p], 0.0) for p in idx]
        qk = [(_dot_nt(q[p].astype(BF16), k16[p]) * decay[p]).astype(BF16) for p in idx]
        e_g = [jnp.exp(g_cum[p]) for p in idx]
        rhs = [jnp.concatenate([v[p] * beta[p], kb[p] * e_g[p]], axis=1).astype(BF16) for p in idx]
        g_last = [g_cum[p][C - 1:C] for p in idx]
        qg = [(q[p] * e_g[p]).astype(BF16) for p in idx]
        kg = [(k[p] * jnp.exp(g_last[p] - g_cum[p])).astype(BF16) for p in idx]
        tinv = _unit_lower_inverses(lmat, eye, level, max_level=(live - 1).bit_length())
        sol = [_dot(tinv[p].astype(BF16), rhs[p]) for p in idx]
        s = [s_ref[g, h] for g, h in pairs]
        s16 = [x.astype(BF16) for x in s]
        u = [sol[p][:, :DK] - _dot(sol[p][:, DK:].astype(BF16), s16[p]) for p in idx]
        u16 = [x.astype(BF16) for x in u]
        o = [_dot(qg[p], s16[p]) + _dot(qk[p], u16[p]) for p in idx]
        s_new = [s[p] * jnp.exp(g_last[p]) + _dot_tn(kg[p], u16[p]) for p in idx]
        for p, (g, h) in enumerate(pairs):
            s_ref[g, h] = s_new[p]
            zh = z_ref[g, :, h * DK:(h + 1) * DK]
            o_ref[g, :, h * DK:(h + 1) * DK] = _rms(o[p], gout_ref[...]) * (zh * _sigmoid(zh))

    ext_ref[:, SUBLANES:SUBLANES + C, :] = qkv_ref[...]
    for g in range(G):
        for blk in range(3 * H):
            prepare_block(g, blk)
    recur([prepare_gates(g) for g in range(G)])
    ext_ref[:, 0:SUBLANES, :] = ext_ref[:, C:C + SUBLANES, :]

    @pl.when(c == pl.num_programs(1) - 1)
    def _():
        sfin_ref[...] = s_ref[...]


def _gdn(xz, ab, buf8, s0, w_conv, a_log, dt_bias, g_out, *, n_seq, seq_rows, valid_len):
    C = GDN_CHUNK
    nc = seq_rows // C
    G = GDN_SEQS_PER_STEP if n_seq % GDN_SEQS_PER_STEP == 0 else 1
    qkv_w = 3 * GDN_HEADS * GDN_DK
    zeros_h = jnp.zeros((GDN_HEADS,), F32)
    alog8 = jnp.concatenate([a_log, zeros_h]).reshape(1, 2 * GDN_HEADS)
    dtb8 = jnp.concatenate([dt_bias, zeros_h]).reshape(1, 2 * GDN_HEADS)
    state = (G, GDN_HEADS, GDN_DK, GDN_DK)
    xz3 = xz.reshape(n_seq, seq_rows, xz.shape[1])
    ab3 = ab.reshape(n_seq, seq_rows, 2 * GDN_HEADS)
    o_a, s_new = pl.pallas_call(
        functools.partial(_gdn_kernel, chunk=C, valid_len=None if valid_len == seq_rows else valid_len),
        grid=(n_seq // G, nc),
        in_specs=[
            pl.BlockSpec((G, C, qkv_w), lambda b, c: (b, c, 0)),
            pl.BlockSpec((G, C, 2 * GDN_HEADS), lambda b, c: (b, c, 0)),
            pl.BlockSpec((G, C, BRANCH_W), lambda b, c: (b, c, COL_Z * LANES // BRANCH_W)),
            pl.BlockSpec((G, SUBLANES, qkv_w), lambda b, c: (b, 0, 0)),
            pl.BlockSpec(state, lambda b, c: (b, 0, 0, 0)),
            pl.BlockSpec((GDN_CONV, qkv_w), lambda b, c: (0, 0)),
            pl.BlockSpec((1, 2 * GDN_HEADS), lambda b, c: (0, 0)),
            pl.BlockSpec((1, 2 * GDN_HEADS), lambda b, c: (0, 0)),
            pl.BlockSpec((1, GDN_DK), lambda b, c: (0, 0)),
        ],
        out_specs=[
            pl.BlockSpec((G, C, BRANCH_W), lambda b, c: (b, c, 0)),
            pl.BlockSpec(state, lambda b, c: (b, 0, 0, 0)),
        ],
        out_shape=[jax.ShapeDtypeStruct((n_seq, seq_rows, BRANCH_W), F32),
                   jax.ShapeDtypeStruct((n_seq, GDN_HEADS, GDN_DK, GDN_DK), F32)],
        scratch_shapes=[pltpu.VMEM((G, SUBLANES + C, qkv_w), F32),
                        pltpu.VMEM(state, F32),
                        pltpu.VMEM((G, C, qkv_w), F32)],
        compiler_params=_params("parallel", "arbitrary"),
        name="gdn",
    )(xz3, ab3, xz3, buf8, s0, w_conv, alog8, dtb8, g_out.reshape(1, GDN_DK))
    return o_a.reshape(n_seq * seq_rows, BRANCH_W), s_new


def _split_components(q):
    lane = lax.broadcasted_iota(jnp.int32, q.shape, 1)
    return jnp.concatenate([jnp.where(lane < DIFF_DQK, q, 0.0), jnp.where(lane >= DIFF_DQK, q, 0.0)], axis=0)


def _diff_prompt_kernel(lam_ref, q_ref, k_ref, v_ref, g_ref, o_ref,
                        k16_ref, v16_ref, qbd_ref, m_ref, l_ref, acc_ref, *, tq, lam_init):
    i = pl.program_id(1)
    dv = 2 * DIFF_DQK

    @pl.when(i == 0)
    def _():
        seq = k16_ref.shape[1]
        for h in range(DIFF_HEADS):
            k16_ref[h] = k_ref[pl.ds(h, seq, stride=DIFF_HEADS), :].astype(BF16)
            v16_ref[h] = v_ref[pl.ds(h, seq, stride=DIFF_HEADS), :].astype(BF16)

    q = q_ref[...] * (DIFF_DQK ** -0.5 * LOG2_E)
    for h in range(DIFF_HEADS):
        qbd_ref[h] = _split_components(q[:, h * dv:(h + 1) * dv]).astype(BF16)
    m_ref[...] = jnp.full(m_ref.shape, -jnp.inf, F32)
    l_ref[...] = jnp.zeros(l_ref.shape, F32)
    acc_ref[...] = jnp.zeros(acc_ref.shape, F32)

    def step(j, masked):
        start = pl.multiple_of(j * tq, tq)
        for h in range(DIFF_HEADS):
            k = k16_ref[h, pl.ds(start, tq), :]
            v = v16_ref[h, pl.ds(start, tq), :]
            s = _dot_nt(qbd_ref[h], k)
            if masked:
                r = lax.broadcasted_iota(jnp.int32, (tq, tq), 0)
                cidx = lax.broadcasted_iota(jnp.int32, (tq, tq), 1)
                keep = jnp.concatenate([cidx <= r, cidx <= r], axis=0)
                s = jnp.where(keep, s, -jnp.inf)
            m_old = m_ref[h]
            m_new = jnp.maximum(m_old, jnp.max(s, axis=-1, keepdims=True))
            alpha = jnp.exp2(m_old - m_new)
            p = jnp.exp2(s - jnp.concatenate([m_new] * (tq // dv), axis=1))
            l_ref[h] = l_ref[h] * alpha + jnp.sum(p, axis=-1, keepdims=True)
            acc_ref[h] = acc_ref[h] * alpha + _dot(p.astype(BF16), v)
            m_ref[h] = m_new

    def body(j, carry):
        step(j, False)
        return carry

    lax.fori_loop(0, i, body, 0)
    step(i, True)
    lam = _diff_lambda(lam_ref, lam_init)
    for h in range(DIFF_HEADS):
        o = acc_ref[h] / l_ref[h]
        a = o[:tq] - lam * o[tq:]
        o_ref[:, h * dv:(h + 1) * dv] = _rms(a, g_ref[...]) * (1.0 - lam_init)


def _diff_prompt(proj, k_rows, v_rows, lam_p, g_sub, *, n_seq, seq, tq, lam_init):
    nq = seq // tq
    dv = 2 * DIFF_DQK
    w = DIFF_HEADS * dv
    stat = pltpu.VMEM((DIFF_HEADS, 2 * tq, dv), F32)
    kv16 = pltpu.VMEM((DIFF_HEADS, seq, dv), BF16)
    return pl.pallas_call(
        functools.partial(_diff_prompt_kernel, tq=tq, lam_init=lam_init),
        grid=(n_seq, nq),
        in_specs=[
            pl.BlockSpec((4, DIFF_DQK), lambda b, i: (0, 0)),
            pl.BlockSpec((tq, w), lambda b, i: (b * nq + i, COL_DQ * LANES // w)),
            pl.BlockSpec((seq * DIFF_HEADS, dv), lambda b, i: (b, 0)),
            pl.BlockSpec((seq * DIFF_HEADS, dv), lambda b, i: (b, 0)),
            pl.BlockSpec((1, dv), lambda b, i: (0, 0)),
        ],
        out_specs=pl.BlockSpec((tq, w), lambda b, i: (b * nq + i, 0)),
        out_shape=jax.ShapeDtypeStruct((n_seq * seq, w), F32),
        scratch_shapes=[kv16, kv16, pltpu.VMEM((DIFF_HEADS, 2 * tq, dv), BF16), stat, stat, stat],
        compiler_params=_params("arbitrary", "arbitrary"),
        name="diff_attn_prompt",
    )(lam_p, proj, k_rows, v_rows, g_sub.reshape(1, dv))


def _diff_paged_kernel(pt_ref, lam_ref, q_ref, ks_ref, vs_ref, g_ref, *rest, t_new, n_pg, group, lam_init):
    del pt_ref
    k_pages = rest[:n_pg]
    v_pages = rest[n_pg:2 * n_pg]
    o_ref, qall_ref, m_ref, l_ref, acc_ref = rest[2 * n_pg:]
    T = t_new
    R = 2 * T
    HR = DIFF_HEADS * R
    dv = 2 * DIFF_DQK
    prow = k_pages[0].shape[0]
    step_idx = pl.program_id(1)

    @pl.when(step_idx == 0)
    def _():
        q = q_ref[...] * (DIFF_DQK ** -0.5)
        t_of_row = lax.broadcasted_iota(jnp.int32, (R, 1), 0) % T
        for h in range(DIFF_HEADS):
            sl = slice(h * dv, (h + 1) * dv)
            rows = slice(h * R, (h + 1) * R)
            qbd = _split_components(q[:, sl])
            qall_ref[rows, :] = qbd.astype(BF16)
            ks = ks_ref[pl.ds(h, T, stride=DIFF_HEADS), :]
            vs = vs_ref[pl.ds(h, T, stride=DIFF_HEADS), :]
            scores = []
            for j in range(T):
                sj = jnp.sum(qbd * ks[j:j + 1], axis=-1, keepdims=True)
                scores.append(jnp.where(t_of_row >= j, sj, -jnp.inf))
            m = scores[0]
            for j in range(1, T):
                m = jnp.maximum(m, scores[j])
            den = jnp.zeros((R, 1), F32)
            acc = jnp.zeros((R, dv), F32)
            for j in range(T):
                pj = jnp.exp(scores[j] - m)
                den = den + pj
                acc = acc + pj * vs[j:j + 1]
            m_ref[rows, :] = jnp.broadcast_to(m, (R, dv))
            l_ref[rows, :] = jnp.broadcast_to(den, (R, dv))
            acc_ref[rows, :] = acc

    qall = qall_ref[...]
    row_head = lax.broadcasted_iota(jnp.int32, (HR, prow), 0) // R
    col_head = lax.broadcasted_iota(jnp.int32, (HR, prow), 1) % DIFF_HEADS
    bias = jnp.where(row_head == col_head, 0.0, -jnp.inf)

    parts = []
    for g0 in range(0, n_pg, group):
        tiles = [_dot_nt(qall, k_pages[n][...].astype(BF16)) + bias for n in range(g0, g0 + group)]
        mx = tiles[0]
        for t in tiles[1:]:
            mx = jnp.maximum(mx, t)
        m_g = jnp.max(mx, axis=-1, keepdims=True)
        l_g = None
        acc_g = None
        for n, t in zip(range(g0, g0 + group), tiles):
            p = jnp.exp(t - m_g)
            ps = jnp.sum(p, axis=-1, keepdims=True)
            pv = _dot(p.astype(BF16), v_pages[n][...].astype(BF16))
            l_g = ps if l_g is None else l_g + ps
            acc_g = pv if acc_g is None else acc_g + pv
        parts.append((m_g, l_g, acc_g))

    m_old = m_ref[...]
    m_new = m_old
    for m_g, _, _ in parts:
        m_new = jnp.maximum(m_new, m_g)
    a_old = jnp.exp(m_old - m_new)
    l_new = l_ref[...] * a_old
    acc_new = acc_ref[...] * a_old
    for m_g, l_g, acc_g in parts:
        wgt = jnp.exp(m_g - m_new)
        l_new = l_new + l_g * wgt
        acc_new = acc_new + acc_g * wgt
    m_ref[...] = m_new
    l_ref[...] = l_new
    acc_ref[...] = acc_new

    @pl.when(step_idx == pl.num_programs(1) - 1)
    def _():
        lam = _diff_lambda(lam_ref, lam_init)
        o = acc_ref[...] / l_ref[...]
        for h in range(DIFF_HEADS):
            a = o[h * R:h * R + T] - lam * o[h * R + T:(h + 1) * R]
            o_ref[:, h * dv:(h + 1) * dv] = _rms(a, g_ref[...]) * (1.0 - lam_init)


def _diff_paged(proj, k_rows, v_rows, lam_p, g_sub, cache_k, cache_v, page_table, *, layer, t_new, lam_init):
    depth, n_pool, page, heads, kw = cache_k.shape
    n_seq, n_pages = page_table.shape
    assert heads == DIFF_HEADS and kw == 2 * DIFF_DQK and cache_v.shape == cache_k.shape
    n_pg = min(PAGES_PER_STEP, n_pages)
    group = min(PAGE_GROUP, n_pg)
    assert n_pages % n_pg == 0 and n_pg % group == 0
    n_steps = n_pages // n_pg
    dv = 2 * DIFF_DQK
    w = heads * kw
    ck = cache_k.reshape(depth * n_pool, page * heads, kw)
    cv = cache_v.reshape(depth * n_pool, page * heads, kw)
    pt = page_table.reshape(-1)
    base = layer * n_pool

    def page_spec(n):
        return pl.BlockSpec((None, page * heads, kw),
                            lambda b, s, pt: (base + pt[b * n_pages + s * n_pg + n], 0, 0))

    stat = pltpu.VMEM((DIFF_HEADS * 2 * t_new, dv), F32)
    grid_spec = pltpu.PrefetchScalarGridSpec(
        num_scalar_prefetch=1,
        grid=(n_seq, n_steps),
        in_specs=[
            pl.BlockSpec((4, DIFF_DQK), lambda b, s, pt: (0, 0)),
            pl.BlockSpec((t_new, w), lambda b, s, pt: (b, COL_DQ * LANES // w)),
            pl.BlockSpec((t_new * heads, kw), lambda b, s, pt: (b, 0)),
            pl.BlockSpec((t_new * heads, kw), lambda b, s, pt: (b, 0)),
            pl.BlockSpec((1, dv), lambda b, s, pt: (0, 0)),
        ] + [page_spec(n) for n in range(n_pg)] + [page_spec(n) for n in range(n_pg)],
        out_specs=pl.BlockSpec((t_new, w), lambda b, s, pt: (b, 0)),
        scratch_shapes=[pltpu.VMEM((DIFF_HEADS * 2 * t_new, dv), BF16), stat, stat, stat],
    )
    return pl.pallas_call(
        functools.partial(_diff_paged_kernel, t_new=t_new, n_pg=n_pg, group=group, lam_init=lam_init),
        grid_spec=grid_spec,
        out_shape=jax.ShapeDtypeStruct((n_seq * t_new, w), F32),
        compiler_params=_params("parallel", "arbitrary"),
        name="diff_attn_paged",
    )(pt, lam_p, proj, k_rows, v_rows, g_sub.reshape(1, dv), *([ck] * n_pg), *([cv] * n_pg))


def _cross_kernel(q_ref, mk_ref, mv_ref, o_ref):
    q = q_ref[...] * (XA_DH ** -0.5)
    for h in range(XA_HEADS):
        sl = slice(h * XA_DH, (h + 1) * XA_DH)
        s = _dot_nt(q[:, sl].astype(BF16), mk_ref[:, sl].astype(BF16))
        p = jnp.exp(s - jnp.max(s, axis=-1, keepdims=True))
        den = jnp.sum(p, axis=-1, keepdims=True)
        o_ref[:, sl] = _dot(p.astype(BF16), mv_ref[:, sl].astype(BF16)) / den


def _cross_attn(proj, mk, mv, mk_col, mv_col, *, n_seq, seq, n_mem, tq):
    nq = seq // tq
    w = XA_HEADS * XA_DH
    return pl.pallas_call(
        _cross_kernel,
        grid=(n_seq, nq),
        in_specs=[
            pl.BlockSpec((tq, w), lambda b, i: (b * nq + i, COL_XQ * LANES // w)),
            pl.BlockSpec((n_mem, w), lambda b, i: (b, mk_col)),
            pl.BlockSpec((n_mem, w), lambda b, i: (b, mv_col)),
        ],
        out_specs=pl.BlockSpec((tq, w), lambda b, i: (b * nq + i, 0)),
        out_shape=jax.ShapeDtypeStruct((n_seq * seq, w), F32),
        compiler_params=_params("parallel", "arbitrary"),
        name="cross_attn",
    )(proj, mk, mv)


def _merge_kernel(x_ref, oa_ref, ob_ref, oc_ref, ga_ref, gb_ref, gc_ref, wb_ref, wo_ref, g_ref, o_ref):
    merged = None
    for n, (br, gl) in enumerate(((oa_ref, ga_ref), (ob_ref, gb_ref), (oc_ref, gc_ref))):
        up = _dot(br[...].astype(BF16), wb_ref[n])
        term = _sigmoid(gl[...]) * up
        merged = term if merged is None else merged + term
    y = _dot(merged.astype(BF16), wo_ref[...])
    o_ref[...] = x_ref[...] + _rms(y, g_ref[...])


def _merge(x, oa, ob, oc, proj, w_branch, w_out, g_post, *, tm):
    m, d = x.shape
    gate0 = COL_GATE * LANES // d
    row = lambda i: (i, 0)
    return pl.pallas_call(
        _merge_kernel,
        grid=(m // tm,),
        in_specs=[
            pl.BlockSpec((tm, d), row),
            pl.BlockSpec((tm, BRANCH_W), row), pl.BlockSpec((tm, BRANCH_W), row), pl.BlockSpec((tm, BRANCH_W), row),
            pl.BlockSpec((tm, d), lambda i: (i, gate0)),
            pl.BlockSpec((tm, d), lambda i: (i, gate0 + 1)),
            pl.BlockSpec((tm, d), lambda i: (i, gate0 + 2)),
            pl.BlockSpec((N_BRANCH, BRANCH_W, d), lambda i: (0, 0, 0)),
            pl.BlockSpec((d, d), lambda i: (0, 0)),
            pl.BlockSpec((1, d), lambda i: (0, 0)),
        ],
        out_specs=pl.BlockSpec((tm, d), row),
        out_shape=jax.ShapeDtypeStruct((m, d), F32),
        compiler_params=_params("parallel"),
        name="merge",
    )(x, oa, ob, oc, proj, proj, proj, w_branch, w_out, g_post.reshape(1, d))


def _mlp_kernel(x_ref, gpre_ref, w1_ref, w2_ref, gpost_ref, o_ref, h_ref, acc_ref):
    j = pl.program_id(1)

    @pl.when(j == 0)
    def _():
        h_ref[...] = _rms(x_ref[...], gpre_ref[...]).astype(BF16)
        acc_ref[...] = jnp.zeros(acc_ref.shape, F32)

    a = jnp.square(jnp.maximum(_dot(h_ref[...], w1_ref[...]), 0.0))
    acc_ref[...] += _dot(a.astype(BF16), w2_ref[...])

    @pl.when(j == pl.num_programs(1) - 1)
    def _():
        o_ref[...] = x_ref[...] + _rms(acc_ref[...], gpost_ref[...])


def _mlp(x, g_pre, w1, w2, g_post, *, tm, tf):
    m, d = x.shape
    dff = w1.shape[1]
    return pl.pallas_call(
        _mlp_kernel,
        grid=(m // tm, dff // tf),
        in_specs=[
            pl.BlockSpec((tm, d), lambda i, j: (i, 0)),
            pl.BlockSpec((1, d), lambda i, j: (0, 0)),
            pl.BlockSpec((d, tf), lambda i, j: (0, j)),
            pl.BlockSpec((tf, d), lambda i, j: (j, 0)),
            pl.BlockSpec((1, d), lambda i, j: (0, 0)),
        ],
        out_specs=pl.BlockSpec((tm, d), lambda i, j: (i, 0)),
        out_shape=jax.ShapeDtypeStruct((m, d), F32),
        scratch_shapes=[pltpu.VMEM((tm, d), BF16), pltpu.VMEM((tm, d), F32)],
        compiler_params=_params("parallel", "arbitrary"),
        name="mlp",
    )(x, g_pre.reshape(1, d), w1, w2, g_post.reshape(1, d))


def _row_tile(m, cap):
    t = min(m, cap)
    assert m % t == 0
    return t


def _trunk_layer(x, w, lam_init, *, n_seq, seq, buf8, s0, mem, attend):
    m, d = x.shape
    proj, ab, k_rows, v_rows = _in_proj(x, w["g_pre_mix"], w["w_main"], w["w_ab"], tm=_row_tile(m, 1024))

    C = GDN_CHUNK
    seq_pad = -(-seq // C) * C
    if seq_pad == seq:
        xz, ab_p = proj, ab
    else:
        xz = jnp.pad(proj[:, :(COL_Z + 4) * LANES].reshape(n_seq, seq, -1), ((0, 0), (0, seq_pad - seq), (0, 0)))
        xz = xz.reshape(n_seq * seq_pad, -1)
        ab_p = jnp.pad(ab.reshape(n_seq, seq, -1), ((0, 0), (0, seq_pad - seq), (0, 0))).reshape(n_seq * seq_pad, -1)
    o_a, s_new = _gdn(xz, ab_p, buf8, s0, w["w_conv"], w["gdn_a_log"], w["gdn_dt_bias"], w["g_gdn_out"],
                      n_seq=n_seq, seq_rows=seq_pad, valid_len=seq)
    if seq_pad != seq:
        o_a = o_a.reshape(n_seq, seq_pad, -1)[:, :seq].reshape(m, -1)

    o_b = attend(proj, k_rows, v_rows)
    mk, mv, mk_col, mv_col, n_mem = mem
    o_c = _cross_attn(proj, mk, mv, mk_col, mv_col, n_seq=n_seq, seq=seq, n_mem=n_mem, tq=_row_tile(seq, 512))

    x = _merge(x, o_a, o_b, o_c, proj, w["w_branch"], w["w_out"], w["g_post_mix"], tm=_row_tile(m, 512))
    x = _mlp(x, w["g_pre_mlp"], w["w_ff1"], w["w_ff2"], w["g_post_mlp"], tm=_row_tile(m, 1024), tf=1024)
    return x, proj, k_rows, v_rows, s_new


def kernel(x_prompt, x_sample, mem_prompt, cache_diff_k, cache_diff_v, page_table, state_gdn, cache_gdn_conv,
           cache_mem_k, cache_mem_v, g_pre_mix, w_in, w_conv, gdn_a_log, gdn_dt_bias, g_gdn_out, diff_lambda,
           g_diff_sub, g_mem, w_mem_k, w_mem_v, w_branch, w_out, g_post_mix, g_pre_mlp, w_ff1, w_ff2, g_post_mlp):
    bp, sp, d = x_prompt.shape
    bs, ts, _ = x_sample.shape
    depth = w_in.shape[0]
    n_mem = mem_prompt.shape[1]
    qkv_w = 3 * GDN_HEADS * GDN_DK
    assert sp >= GDN_CONV - 1 and ts >= GDN_CONV - 1
    xa_w = XA_HEADS * XA_DH
    dk_w = DIFF_HEADS * 2 * DIFF_DQK

    xp = x_prompt.reshape(bp * sp, d)
    xs = x_sample.reshape(bs * ts, d)
    memf = mem_prompt.reshape(bp * n_mem, d)

    c_a = qkv_w + BRANCH_W
    c_dq = c_a + 2 * GDN_HEADS
    c_dk = c_dq + dk_w
    c_xq = c_dk + 2 * dk_w

    outs = {k: [] for k in ("kp", "vp", "sp", "cp", "mk", "mv", "ks", "vs", "ss", "cs")}
    zero_buf = jnp.zeros((bp, SUBLANES, qkv_w), F32)
    zero_state = jnp.zeros((bp, GDN_HEADS, GDN_DK, GDN_DK), F32)
    for l in range(depth):
        lam_init = 0.8 - 0.6 * math.exp(-0.3 * l)
        w = {
            "g_pre_mix": g_pre_mix[l],
            "w_main": jnp.concatenate([w_in[l][:, :c_a], w_in[l][:, c_dq:c_dk], w_in[l][:, c_xq:],
                                       w_in[l][:, c_dk:c_xq]], axis=1).astype(BF16),
            "w_ab": w_in[l][:, c_a:c_dq].astype(BF16),
            "w_conv": w_conv[l], "gdn_a_log": gdn_a_log[l], "gdn_dt_bias": gdn_dt_bias[l],
            "g_gdn_out": g_gdn_out[l],
            "w_branch": w_branch[l].astype(BF16), "w_out": w_out[l].astype(BF16), "g_post_mix": g_post_mix[l],
            "g_pre_mlp": g_pre_mlp[l], "w_ff1": w_ff1[l].astype(BF16), "w_ff2": w_ff2[l].astype(BF16),
            "g_post_mlp": g_post_mlp[l],
        }
        lam_p = diff_lambda[l]
        g_sub = g_diff_sub[l]

        w_mem = jnp.concatenate([w_mem_k[l], w_mem_v[l]], axis=1).astype(BF16)
        memkv = _norm_matmul(memf, g_mem[l], w_mem, tm=_row_tile(bp * n_mem, 1024), tn=xa_w)
        attend_p = functools.partial(_diff_prompt, lam_p=lam_p, g_sub=g_sub, n_seq=bp, seq=sp,
                                     tq=_row_tile(sp, 256), lam_init=lam_init)
        xp, proj_p, k_p, v_p, s_p = _trunk_layer(xp, w, lam_init, n_seq=bp, seq=sp, buf8=zero_buf, s0=zero_state,
                                                 mem=(memkv, memkv, 0, 1, n_mem), attend=attend_p)
        proj3 = proj_p.reshape(bp, sp, -1)
        outs["kp"].append(k_p.reshape(bp, sp, DIFF_HEADS, -1))
        outs["vp"].append(v_p.reshape(bp, sp, DIFF_HEADS, -1))
        outs["sp"].append(s_p)
        outs["cp"].append(proj3[:, sp - (GDN_CONV - 1):, :qkv_w])
        outs["mk"].append(memkv[:, :xa_w].reshape(bp, n_mem, XA_HEADS, XA_DH))
        outs["mv"].append(memkv[:, xa_w:].reshape(bp, n_mem, XA_HEADS, XA_DH))

        buf8 = jnp.pad(cache_gdn_conv[l], ((0, 0), (SUBLANES - (GDN_CONV - 1), 0), (0, 0)))
        attend_s = functools.partial(_diff_paged, lam_p=lam_p, g_sub=g_sub, cache_k=cache_diff_k,
                                     cache_v=cache_diff_v, page_table=page_table, layer=l, t_new=ts,
                                     lam_init=lam_init)
        mk_s = cache_mem_k[l].reshape(bs * n_mem, xa_w)
        mv_s = cache_mem_v[l].reshape(bs * n_mem, xa_w)
        xs, proj_s, k_s, v_s, s_s = _trunk_layer(xs, w, lam_init, n_seq=bs, seq=ts, buf8=buf8, s0=state_gdn[l],
                                                 mem=(mk_s, mv_s, 0, 0, n_mem), attend=attend_s)
        proj3 = proj_s.reshape(bs, ts, -1)
        outs["ks"].append(k_s.reshape(bs, ts, DIFF_HEADS, -1))
        outs["vs"].append(v_s.reshape(bs, ts, DIFF_HEADS, -1))
        outs["ss"].append(s_s)
        outs["cs"].append(proj3[:, ts - (GDN_CONV - 1):, :qkv_w])

    st = lambda k: jnp.stack(outs[k])
    return (xp.reshape(bp, sp, d), xs.reshape(bs, ts, d),
            st("kp"), st("vp"), st("sp"), st("cp"), st("mk"), st("mv"),
            st("ks"), st("vs"), st("ss"), st("cs"))
```

```python
import functools
import math

import jax
import jax.numpy as jnp
from jax import lax
from jax.experimental import pallas as pl
from jax.experimental.pallas import tpu as pltpu

F32 = jnp.float32
BF16 = jnp.bfloat16
RMS_EPS = 1e-6
L2_EPS = 1e-6
LANES = 128
SUBLANES = 8
VMEM_LIMIT_BYTES = 48 * 1024 * 1024

GDN_HEADS = 4
GDN_DK = 128
GDN_CONV = 4
GDN_CHUNK = 128
GDN_SEQS_PER_STEP = 4
DIFF_HEADS = 4
DIFF_DQK = 64
XA_HEADS = 4
XA_DH = 128
N_BRANCH = 3
BRANCH_W = 512
PAGES_PER_STEP = 32
PAGE_GROUP = 4
LOG2_E = math.log2(math.e)

COL_QKV, COL_Z, COL_DQ, COL_XQ, COL_GATE = 0, 12, 16, 20, 24
MAIN_COLS = 48 * LANES


def _params(*sem):
    return pltpu.CompilerParams(dimension_semantics=sem, vmem_limit_bytes=VMEM_LIMIT_BYTES)


def _rms(x, g):
    return x * lax.rsqrt(jnp.mean(x * x, axis=-1, keepdims=True) + RMS_EPS) * g


def _sigmoid(x):
    return 1.0 / (1.0 + jnp.exp(-x))


def _dot(a, b, precision=None):
    return jnp.dot(a, b, preferred_element_type=F32, precision=precision)


def _dot_nt(a, b, precision=None):
    return lax.dot_general(a, b, (((1,), (1,)), ((), ())), preferred_element_type=F32, precision=precision)


def _dot_tn(a, b, precision=None):
    return lax.dot_general(a, b, (((0,), (0,)), ((), ())), preferred_element_type=F32, precision=precision)


def _unit_lower_inverses(lmats, eye, level, max_level):
    ts = [eye - jnp.where(level <= 1, lm, 0.0) for lm in lmats]
    for k in range(2, max_level + 1):
        t16s = [t.astype(BF16) for t in ts]
        xs = [_dot(jnp.where(level == k, lm, 0.0).astype(BF16), t16) for lm, t16 in zip(lmats, t16s)]
        ts = [t - _dot(t16, x.astype(BF16)) for t, t16, x in zip(ts, t16s, xs)]
    return ts


def _diff_lambda(lam_ref, lam_init):
    lp = lam_ref[...]
    a = jnp.sum(lp[0:1] * lp[1:2], axis=-1, keepdims=True)
    b = jnp.sum(lp[2:3] * lp[3:4], axis=-1, keepdims=True)
    return jnp.exp(a) - jnp.exp(b) + lam_init


def _mem_proj_kernel(x_ref, g_ref, w_ref, k_ref, v_ref):
    tm = x_ref.shape[0]
    res = _dot(_rms(x_ref[...], g_ref[...]).astype(BF16), w_ref[...])
    for hd in range(XA_HEADS):
        k_ref[pl.ds(hd, tm, stride=XA_HEADS), :] = res[:, hd * XA_DH:(hd + 1) * XA_DH]
        v_ref[pl.ds(hd, tm, stride=XA_HEADS), :] = res[:, (XA_HEADS + hd) * XA_DH:(XA_HEADS + hd + 1) * XA_DH]


def _mem_proj(x, g, w, *, tm):
    m, d = x.shape
    n = w.shape[1]
    rows = jax.ShapeDtypeStruct((m * XA_HEADS, XA_DH), F32)
    return pl.pallas_call(
        _mem_proj_kernel,
        grid=(m // tm,),
        in_specs=[pl.BlockSpec((tm, d), lambda i: (i, 0)),
                  pl.BlockSpec((1, d), lambda i: (0, 0)),
                  pl.BlockSpec((d, n), lambda i: (0, 0))],
        out_specs=[pl.BlockSpec((tm * XA_HEADS, XA_DH), lambda i: (i, 0)),
                   pl.BlockSpec((tm * XA_HEADS, XA_DH), lambda i: (i, 0))],
        out_shape=[rows, rows],
        compiler_params=_params("parallel"),
        name="mem_proj",
    )(x, g.reshape(1, d), w)


def _in_proj_kernel(x_ref, g_ref, w_ref, ws_ref, o_ref, os_ref, k_ref, v_ref, h_ref, *, n_main):
    j = pl.program_id(1)
    tm = x_ref.shape[0]
    kw = 2 * DIFF_DQK

    @pl.when(j == 0)
    def _():
        h = _rms(x_ref[...], g_ref[...]).astype(BF16)
        h_ref[...] = h
        os_ref[...] = _dot(h, ws_ref[...])

    tn = o_ref.shape[1]
    res = _dot(h_ref[...], w_ref[:, pl.ds(pl.multiple_of(j * tn, tn), tn)])

    @pl.when(j < n_main)
    def _():
        o_ref[...] = res

    @pl.when(j == n_main)
    def _():
        for hd in range(DIFF_HEADS):
            k_ref[pl.ds(hd, tm, stride=DIFF_HEADS), :] = res[:, hd * kw:(hd + 1) * kw]
            v_ref[pl.ds(hd, tm, stride=DIFF_HEADS), :] = res[:, (DIFF_HEADS + hd) * kw:(DIFF_HEADS + hd + 1) * kw]


def _in_proj(x, g, w, w_side, *, tm):
    m, d = x.shape
    tn = 2 * DIFF_HEADS * 2 * DIFF_DQK
    n_main = MAIN_COLS // tn
    assert w.shape[1] == MAIN_COLS + tn and MAIN_COLS % tn == 0
    ns = w_side.shape[1]
    kw = 2 * DIFF_DQK
    return pl.pallas_call(
        functools.partial(_in_proj_kernel, n_main=n_main),
        grid=(m // tm, n_main + 1),
        in_specs=[pl.BlockSpec((tm, d), lambda i, j: (i, 0)),
                  pl.BlockSpec((1, d), lambda i, j: (0, 0)),
                  pl.BlockSpec((d, MAIN_COLS + tn), lambda i, j: (0, 0), pipeline_mode=pl.Buffered(1)),
                  pl.BlockSpec((d, ns), lambda i, j: (0, 0))],
        out_specs=[pl.BlockSpec((tm, tn), lambda i, j: (i, jnp.minimum(j, n_main - 1))),
                   pl.BlockSpec((tm, ns), lambda i, j: (i, 0)),
                   pl.BlockSpec((tm * DIFF_HEADS, kw), lambda i, j: (i, 0)),
                   pl.BlockSpec((tm * DIFF_HEADS, kw), lambda i, j: (i, 0))],
        out_shape=[jax.ShapeDtypeStruct((m, MAIN_COLS), F32),
                   jax.ShapeDtypeStruct((m, ns), F32),
                   jax.ShapeDtypeStruct((m * DIFF_HEADS, kw), F32),
                   jax.ShapeDtypeStruct((m * DIFF_HEADS, kw), F32)],
        scratch_shapes=[pltpu.VMEM((tm, d), BF16)],
        compiler_params=_params("parallel", "arbitrary"),
        name="in_proj",
    )(x, g.reshape(1, d), w, w_side)


def _gdn_kernel(qkv_ref, ab_ref, z_ref, buf_ref, s0_ref, wconv_ref, alog_ref, dtb_ref, gout_ref,
                o_ref, sfin_ref, ext_ref, s_ref, prep_ref, *, chunk, valid_len):
    C = chunk
    G = qkv_ref.shape[0]
    H = GDN_HEADS
    DK = GDN_DK
    c = pl.program_id(1)
    hi = lax.Precision.HIGHEST
    live = C if valid_len is None else min(C, -(-valid_len // SUBLANES) * SUBLANES)

    @pl.when(c == 0)
    def _():
        ext_ref[:, 0:SUBLANES, :] = buf_ref[...]
        s_ref[...] = s0_ref[...]

    row = lax.broadcasted_iota(jnp.int32, (C, C), 0)
    col = lax.broadcasted_iota(jnp.int32, (C, C), 1)
    incl = row >= col
    strict = row > col

    def prepare_block(g, blk):
        first = SUBLANES - (GDN_CONV - 1)
        sl = slice(blk * DK, (blk + 1) * DK)
        yb = ext_ref[g, first:first + live, sl] * wconv_ref[0:1, sl]
        for i in range(1, GDN_CONV):
            yb = yb + ext_ref[g, first + i:first + i + live, sl] * wconv_ref[i:i + 1, sl]
        yb = yb * _sigmoid(yb)
        if blk < 2 * H:
            yb = yb * lax.rsqrt(jnp.sum(yb * yb, axis=-1, keepdims=True) + L2_EPS)
        if blk < H:
            yb = yb * (DK ** -0.5)
        prep_ref[g, 0:live, sl] = yb
        if live < C:
            prep_ref[g, live:C, sl] = jnp.zeros((C - live, DK), F32)

    def prepare_gates(g):
        ab = ab_ref[g]
        sp_in = ab + dtb_ref[...]
        softplus = jnp.maximum(sp_in, 0.0) + jnp.log(1.0 + jnp.exp(-jnp.abs(sp_in)))
        g_all = -jnp.exp(alog_ref[...]) * softplus
        beta_all = _sigmoid(ab)
        if valid_len is not None:
            row_ok = (c * C + lax.broadcasted_iota(jnp.int32, (C, 1), 0)) < valid_len
            g_all = jnp.where(row_ok, g_all, 0.0)
            beta_all = jnp.where(row_ok, beta_all, 0.0)
        tril = jnp.where(incl, 1.0, 0.0).astype(F32)
        return _dot(tril, g_all, hi), beta_all

    def recur(gates):
        eye = jnp.where(row == col, 1.0, 0.0).astype(F32)
        level = 32 - lax.clz(row ^ col)
        pairs = [(g, h) for g in range(G) for h in range(H)]
        idx = range(len(pairs))
        q = [prep_ref[g, :, h * DK:(h + 1) * DK] for g, h in pairs]
        k = [prep_ref[g, :, (H + h) * DK:(H + h + 1) * DK] for g, h in pairs]
        v = [prep_ref[g, :, (2 * H + h) * DK:(2 * H + h + 1) * DK] for g, h in pairs]
        beta = [gates[g][1][:, H + h:H + h + 1] for g, h in pairs]
        g_cum = [gates[g][0][:, h:h + 1] for g, h in pairs]
        decay = []
        for p in idx:
            g_cum_b = jnp.broadcast_to(g_cum[p], (C, C))
            decay.append(jnp.where(incl, jnp.exp(jnp.where(incl, g_cum_b - g_cum_b.T, 0.0)), 0.0))
        kb = [k[p] * beta[p] for p in idx]
        k16 = [x.astype(BF16) for x in k]
        lmat = [jnp.where(strict, _dot_nt(kb[p].astype(BF16), k16[p]) * decay[p], 0.0) for p in idx]
        qk = [(_dot_nt(q[p].astype(BF16), k16[p]) * decay[p]).astype(BF16) for p in idx]
        e_g = [jnp.exp(g_cum[p]) for p in idx]
        rhs = [jnp.concatenate([v[p] * beta[p], kb[p] * e_g[p]], axis=1).astype(BF16) for p in idx]
        g_last = [g_cum[p][C - 1:C] for p in idx]
        qg = [(q[p] * e_g[p]).astype(BF16) for p in idx]
        kg = [(k[p] * jnp.exp(g_last[p] - g_cum[p])).astype(BF16) for p in idx]
        tinv = _unit_lower_inverses(lmat, eye, level, max_level=(live - 1).bit_length())
        sol = [_dot(tinv[p].astype(BF16), rhs[p]) for p in idx]
        s = [s_ref[g, h] for g, h in pairs]
        s16 = [x.astype(BF16) for x in s]
        u = [sol[p][:, :DK] - _dot(sol[p][:, DK:].astype(BF16), s16[p]) for p in idx]
        u16 = [x.astype(BF16) for x in u]
        o = [_dot(qg[p], s16[p]) + _dot(qk[p], u16[p]) for p in idx]
        s_new = [s[p] * jnp.exp(g_last[p]) + _dot_tn(kg[p], u16[p]) for p in idx]
        for p, (g, h) in enumerate(pairs):
            s_ref[g, h] = s_new[p]
            zh = z_ref[g, :, h * DK:(h + 1) * DK]
            o_ref[g, :, h * DK:(h + 1) * DK] = _rms(o[p], gout_ref[...]) * (zh * _sigmoid(zh))

    ext_ref[:, SUBLANES:SUBLANES + C, :] = qkv_ref[...]
    for g in range(G):
        for blk in range(3 * H):
            prepare_block(g, blk)
    recur([prepare_gates(g) for g in range(G)])
    ext_ref[:, 0:SUBLANES, :] = ext_ref[:, C:C + SUBLANES, :]

    @pl.when(c == pl.num_programs(1) - 1)
    def _():
        sfin_ref[...] = s_ref[...]


def _gdn(xz, ab, buf8, s0, s0_offset, w_conv, a_log, dt_bias, g_out, *, n_seq, seq_rows, valid_len):
    C = GDN_CHUNK
    nc = seq_rows // C
    G = GDN_SEQS_PER_STEP if (n_seq % GDN_SEQS_PER_STEP == 0 and s0_offset % GDN_SEQS_PER_STEP == 0) else 1
    s0_blk = s0_offset // G
    qkv_w = 3 * GDN_HEADS * GDN_DK
    zeros_h = jnp.zeros((GDN_HEADS,), F32)
    alog8 = jnp.concatenate([a_log, zeros_h]).reshape(1, 2 * GDN_HEADS)
    dtb8 = jnp.concatenate([dt_bias, zeros_h]).reshape(1, 2 * GDN_HEADS)
    state = (G, GDN_HEADS, GDN_DK, GDN_DK)
    xz3 = xz.reshape(n_seq, seq_rows, xz.shape[1])
    ab3 = ab.reshape(n_seq, seq_rows, 2 * GDN_HEADS)
    o_a, s_new = pl.pallas_call(
        functools.partial(_gdn_kernel, chunk=C, valid_len=None if valid_len == seq_rows else valid_len),
        grid=(n_seq // G, nc),
        in_specs=[
            pl.BlockSpec((G, C, qkv_w), lambda b, c: (b, c, 0)),
            pl.BlockSpec((G, C, 2 * GDN_HEADS), lambda b, c: (b, c, 0)),
            pl.BlockSpec((G, C, BRANCH_W), lambda b, c: (b, c, COL_Z * LANES // BRANCH_W)),
            pl.BlockSpec((G, SUBLANES, qkv_w), lambda b, c: (b, 0, 0)),
            pl.BlockSpec(state, lambda b, c: (s0_blk + b, 0, 0, 0)),
            pl.BlockSpec((GDN_CONV, qkv_w), lambda b, c: (0, 0)),
            pl.BlockSpec((1, 2 * GDN_HEADS), lambda b, c: (0, 0)),
            pl.BlockSpec((1, 2 * GDN_HEADS), lambda b, c: (0, 0)),
            pl.BlockSpec((1, GDN_DK), lambda b, c: (0, 0)),
        ],
        out_specs=[
            pl.BlockSpec((G, C, BRANCH_W), lambda b, c: (b, c, 0)),
            pl.BlockSpec(state, lambda b, c: (b, 0, 0, 0)),
        ],
        out_shape=[jax.ShapeDtypeStruct((n_seq, seq_rows, BRANCH_W), F32),
                   jax.ShapeDtypeStruct((n_seq, GDN_HEADS, GDN_DK, GDN_DK), F32)],
        scratch_shapes=[pltpu.VMEM((G, SUBLANES + C, qkv_w), F32),
                        pltpu.VMEM(state, F32),
                        pltpu.VMEM((G, C, qkv_w), F32)],
        compiler_params=_params("parallel", "arbitrary"),
        name="gdn",
    )(xz3, ab3, xz3, buf8, s0, w_conv, alog8, dtb8, g_out.reshape(1, GDN_DK))
    return o_a.reshape(n_seq * seq_rows, BRANCH_W), s_new


def _split_components(q):
    lane = lax.broadcasted_iota(jnp.int32, q.shape, 1)
    return jnp.concatenate([jnp.where(lane < DIFF_DQK, q, 0.0), jnp.where(lane >= DIFF_DQK, q, 0.0)], axis=0)


def _diff_prompt_kernel(lam_ref, q_ref, k_ref, v_ref, g_ref, o_ref,
                        k16_ref, v16_ref, qbd_ref, m_ref, l_ref, acc_ref, *, tq, lam_init):
    i = pl.program_id(1)
    dv = 2 * DIFF_DQK

    @pl.when(i == 0)
    def _():
        seq = k16_ref.shape[1]
        for h in range(DIFF_HEADS):
            k16_ref[h] = k_ref[pl.ds(h, seq, stride=DIFF_HEADS), :].astype(BF16)
            v16_ref[h] = v_ref[pl.ds(h, seq, stride=DIFF_HEADS), :].astype(BF16)

    q = q_ref[...] * (DIFF_DQK ** -0.5 * LOG2_E)
    for h in range(DIFF_HEADS):
        qbd_ref[h] = _split_components(q[:, h * dv:(h + 1) * dv]).astype(BF16)
    m_ref[...] = jnp.full(m_ref.shape, -jnp.inf, F32)
    l_ref[...] = jnp.zeros(l_ref.shape, F32)
    acc_ref[...] = jnp.zeros(acc_ref.shape, F32)

    def step(j, masked):
        start = pl.multiple_of(j * tq, tq)
        for h in range(DIFF_HEADS):
            k = k16_ref[h, pl.ds(start, tq), :]
            v = v16_ref[h, pl.ds(start, tq), :]
            s = _dot_nt(qbd_ref[h], k)
            if masked:
                r = lax.broadcasted_iota(jnp.int32, (tq, tq), 0)
                cidx = lax.broadcasted_iota(jnp.int32, (tq, tq), 1)
                keep = jnp.concatenate([cidx <= r, cidx <= r], axis=0)
                s = jnp.where(keep, s, -jnp.inf)
            m_old = m_ref[h]
            m_new = jnp.maximum(m_old, jnp.max(s, axis=-1, keepdims=True))
            alpha = jnp.exp2(m_old - m_new)
            p = jnp.exp2(s - jnp.concatenate([m_new] * (tq // dv), axis=1))
            l_ref[h] = l_ref[h] * alpha + jnp.sum(p, axis=-1, keepdims=True)
            acc_ref[h] = acc_ref[h] * alpha + _dot(p.astype(BF16), v)
            m_ref[h] = m_new

    def body(j, carry):
        step(j, False)
        return carry

    lax.fori_loop(0, i, body, 0)
    step(i, True)
    lam = _diff_lambda(lam_ref, lam_init)
    for h in range(DIFF_HEADS):
        o = acc_ref[h] / l_ref[h]
        a = o[:tq] - lam * o[tq:]
        o_ref[:, h * dv:(h + 1) * dv] = _rms(a, g_ref[...]) * (1.0 - lam_init)


def _diff_prompt(proj, k_rows, v_rows, lam_p, g_sub, *, n_seq, seq, tq, lam_init):
    nq = seq // tq
    dv = 2 * DIFF_DQK
    w = DIFF_HEADS * dv
    stat = pltpu.VMEM((DIFF_HEADS, 2 * tq, dv), F32)
    kv16 = pltpu.VMEM((DIFF_HEADS, seq, dv), BF16)
    return pl.pallas_call(
        functools.partial(_diff_prompt_kernel, tq=tq, lam_init=lam_init),
        grid=(n_seq, nq),
        in_specs=[
            pl.BlockSpec((4, DIFF_DQK), lambda b, i: (0, 0)),
            pl.BlockSpec((tq, w), lambda b, i: (b * nq + i, COL_DQ * LANES // w)),
            pl.BlockSpec((seq * DIFF_HEADS, dv), lambda b, i: (b, 0)),
            pl.BlockSpec((seq * DIFF_HEADS, dv), lambda b, i: (b, 0)),
            pl.BlockSpec((1, dv), lambda b, i: (0, 0)),
        ],
        out_specs=pl.BlockSpec((tq, w), lambda b, i: (b * nq + i, 0)),
        out_shape=jax.ShapeDtypeStruct((n_seq * seq, w), F32),
        scratch_shapes=[kv16, kv16, pltpu.VMEM((DIFF_HEADS, 2 * tq, dv), BF16), stat, stat, stat],
        compiler_params=_params("arbitrary", "arbitrary"),
        name="diff_attn_prompt",
    )(lam_p, proj, k_rows, v_rows, g_sub.reshape(1, dv))


def _diff_paged_kernel(pt_ref, lam_ref, q_ref, ks_ref, vs_ref, g_ref, *rest, t_new, n_pg, group, lam_init):
    del pt_ref
    k_pages = rest[:n_pg]
    v_pages = rest[n_pg:2 * n_pg]
    o_ref, qall_ref, m_ref, l_ref, acc_ref = rest[2 * n_pg:]
    T = t_new
    R = 2 * T
    HR = DIFF_HEADS * R
    dv = 2 * DIFF_DQK
    prow = k_pages[0].shape[0]
    step_idx = pl.program_id(1)

    @pl.when(step_idx == 0)
    def _():
        q = q_ref[...] * (DIFF_DQK ** -0.5)
        t_of_row = lax.broadcasted_iota(jnp.int32, (R, 1), 0) % T
        for h in range(DIFF_HEADS):
            sl = slice(h * dv, (h + 1) * dv)
            rows = slice(h * R, (h + 1) * R)
            qbd = _split_components(q[:, sl])
            qall_ref[rows, :] = qbd.astype(BF16)
            ks = ks_ref[pl.ds(h, T, stride=DIFF_HEADS), :]
            vs = vs_ref[pl.ds(h, T, stride=DIFF_HEADS), :]
            scores = []
            for j in range(T):
                sj = jnp.sum(qbd * ks[j:j + 1], axis=-1, keepdims=True)
                scores.append(jnp.where(t_of_row >= j, sj, -jnp.inf))
            m = scores[0]
            for j in range(1, T):
                m = jnp.maximum(m, scores[j])
            den = jnp.zeros((R, 1), F32)
            acc = jnp.zeros((R, dv), F32)
            for j in range(T):
                pj = jnp.exp(scores[j] - m)
                den = den + pj
                acc = acc + pj * vs[j:j + 1]
            m_ref[rows, :] = jnp.broadcast_to(m, (R, dv))
            l_ref[rows, :] = jnp.broadcast_to(den, (R, dv))
            acc_ref[rows, :] = acc

    qall = qall_ref[...]
    row_head = lax.broadcasted_iota(jnp.int32, (HR, prow), 0) // R
    col_head = lax.broadcasted_iota(jnp.int32, (HR, prow), 1) % DIFF_HEADS
    bias = jnp.where(row_head == col_head, 0.0, -jnp.inf)

    parts = []
    for g0 in range(0, n_pg, group):
        tiles = [_dot_nt(qall, k_pages[n][...].astype(BF16)) + bias for n in range(g0, g0 + group)]
        mx = tiles[0]
        for t in tiles[1:]:
            mx = jnp.maximum(mx, t)
        m_g = jnp.max(mx, axis=-1, keepdims=True)
        l_g = None
        acc_g = None
        for n, t in zip(range(g0, g0 + group), tiles):
            p = jnp.exp(t - m_g)
            ps = jnp.sum(p, axis=-1, keepdims=True)
            pv = _dot(p.astype(BF16), v_pages[n][...].astype(BF16))
            l_g = ps if l_g is None else l_g + ps
            acc_g = pv if acc_g is None else acc_g + pv
        parts.append((m_g, l_g, acc_g))

    m_old = m_ref[...]
    m_new = m_old
    for m_g, _, _ in parts:
        m_new = jnp.maximum(m_new, m_g)
    a_old = jnp.exp(m_old - m_new)
    l_new = l_ref[...] * a_old
    acc_new = acc_ref[...] * a_old
    for m_g, l_g, acc_g in parts:
        wgt = jnp.exp(m_g - m_new)
        l_new = l_new + l_g * wgt
        acc_new = acc_new + acc_g * wgt
    m_ref[...] = m_new
    l_ref[...] = l_new
    acc_ref[...] = acc_new

    @pl.when(step_idx == pl.num_programs(1) - 1)
    def _():
        lam = _diff_lambda(lam_ref, lam_init)
        o = acc_ref[...] / l_ref[...]
        for h in range(DIFF_HEADS):
            a = o[h * R:h * R + T] - lam * o[h * R + T:(h + 1) * R]
            o_ref[:, h * dv:(h + 1) * dv] = _rms(a, g_ref[...]) * (1.0 - lam_init)


def _diff_paged(proj, k_rows, v_rows, lam_p, g_sub, cache_k, cache_v, page_table, *, layer, t_new, lam_init):
    depth, n_pool, page, heads, kw = cache_k.shape
    n_seq, n_pages = page_table.shape
    assert heads == DIFF_HEADS and kw == 2 * DIFF_DQK and cache_v.shape == cache_k.shape
    n_pg = min(PAGES_PER_STEP, n_pages)
    group = min(PAGE_GROUP, n_pg)
    assert n_pages % n_pg == 0 and n_pg % group == 0
    n_steps = n_pages // n_pg
    dv = 2 * DIFF_DQK
    w = heads * kw
    ck = cache_k.reshape(depth * n_pool, page * heads, kw)
    cv = cache_v.reshape(depth * n_pool, page * heads, kw)
    pt = page_table.reshape(-1)
    base = layer * n_pool

    def page_spec(n):
        return pl.BlockSpec((None, page * heads, kw),
                            lambda b, s, pt: (base + pt[b * n_pages + s * n_pg + n], 0, 0))

    stat = pltpu.VMEM((DIFF_HEADS * 2 * t_new, dv), F32)
    grid_spec = pltpu.PrefetchScalarGridSpec(
        num_scalar_prefetch=1,
        grid=(n_seq, n_steps),
        in_specs=[
            pl.BlockSpec((4, DIFF_DQK), lambda b, s, pt: (0, 0)),
            pl.BlockSpec((t_new, w), lambda b, s, pt: (b, COL_DQ * LANES // w)),
            pl.BlockSpec((t_new * heads, kw), lambda b, s, pt: (b, 0)),
            pl.BlockSpec((t_new * heads, kw), lambda b, s, pt: (b, 0)),
            pl.BlockSpec((1, dv), lambda b, s, pt: (0, 0)),
        ] + [page_spec(n) for n in range(n_pg)] + [page_spec(n) for n in range(n_pg)],
        out_specs=pl.BlockSpec((t_new, w), lambda b, s, pt: (b, 0)),
        scratch_shapes=[pltpu.VMEM((DIFF_HEADS * 2 * t_new, dv), BF16), stat, stat, stat],
    )
    return pl.pallas_call(
        functools.partial(_diff_paged_kernel, t_new=t_new, n_pg=n_pg, group=group, lam_init=lam_init),
        grid_spec=grid_spec,
        out_shape=jax.ShapeDtypeStruct((n_seq * t_new, w), F32),
        compiler_params=_params("parallel", "arbitrary"),
        name="diff_attn_paged",
    )(pt, lam_p, proj, k_rows, v_rows, g_sub.reshape(1, dv), *([ck] * n_pg), *([cv] * n_pg))


def _cross_kernel(q_ref, mk_ref, mv_ref, o_ref):
    q = q_ref[...] * (XA_DH ** -0.5)
    n_mem = mk_ref.shape[0] // XA_HEADS
    for h in range(XA_HEADS):
        sl = slice(h * XA_DH, (h + 1) * XA_DH)
        mk = mk_ref[pl.ds(h, n_mem, stride=XA_HEADS), :].astype(BF16)
        mv = mv_ref[pl.ds(h, n_mem, stride=XA_HEADS), :].astype(BF16)
        s = _dot_nt(q[:, sl].astype(BF16), mk)
        p = jnp.exp(s - jnp.max(s, axis=-1, keepdims=True))
        den = jnp.sum(p, axis=-1, keepdims=True)
        o_ref[:, sl] = _dot(p.astype(BF16), mv) / den


def _cross_attn(proj, mk_rows, mv_rows, seq_offset, *, n_seq, seq, n_mem, tq):
    nq = seq // tq
    w = XA_HEADS * XA_DH
    return pl.pallas_call(
        _cross_kernel,
        grid=(n_seq, nq),
        in_specs=[
            pl.BlockSpec((tq, w), lambda b, i: (b * nq + i, COL_XQ * LANES // w)),
            pl.BlockSpec((n_mem * XA_HEADS, XA_DH), lambda b, i: (seq_offset + b, 0)),
            pl.BlockSpec((n_mem * XA_HEADS, XA_DH), lambda b, i: (seq_offset + b, 0)),
        ],
        out_specs=pl.BlockSpec((tq, w), lambda b, i: (b * nq + i, 0)),
        out_shape=jax.ShapeDtypeStruct((n_seq * seq, w), F32),
        compiler_params=_params("parallel", "arbitrary"),
        name="cross_attn",
    )(proj, mk_rows, mv_rows)


def _merge_kernel(x_ref, oa_ref, ob_ref, oc_ref, ga_ref, gb_ref, gc_ref, wb_ref, wo_ref, g_ref, o_ref):
    merged = None
    for n, (br, gl) in enumerate(((oa_ref, ga_ref), (ob_ref, gb_ref), (oc_ref, gc_ref))):
        up = _dot(br[...].astype(BF16), wb_ref[n])
        term = _sigmoid(gl[...]) * up
        merged = term if merged is None else merged + term
    y = _dot(merged.astype(BF16), wo_ref[...])
    o_ref[...] = x_ref[...] + _rms(y, g_ref[...])


def _merge(x, oa, ob, oc, proj, w_branch, w_out, g_post, *, tm):
    m, d = x.shape
    gate0 = COL_GATE * LANES // d
    row = lambda i: (i, 0)
    return pl.pallas_call(
        _merge_kernel,
        grid=(m // tm,),
        in_specs=[
            pl.BlockSpec((tm, d), row),
            pl.BlockSpec((tm, BRANCH_W), row), pl.BlockSpec((tm, BRANCH_W), row), pl.BlockSpec((tm, BRANCH_W), row),
            pl.BlockSpec((tm, d), lambda i: (i, gate0)),
            pl.BlockSpec((tm, d), lambda i: (i, gate0 + 1)),
            pl.BlockSpec((tm, d), lambda i: (i, gate0 + 2)),
            pl.BlockSpec((N_BRANCH, BRANCH_W, d), lambda i: (0, 0, 0)),
            pl.BlockSpec((d, d), lambda i: (0, 0)),
            pl.BlockSpec((1, d), lambda i: (0, 0)),
        ],
        out_specs=pl.BlockSpec((tm, d), row),
        out_shape=jax.ShapeDtypeStruct((m, d), F32),
        compiler_params=_params("parallel"),
        name="merge",
    )(x, oa, ob, oc, proj, proj, proj, w_branch, w_out, g_post.reshape(1, d))


def _mlp_kernel(x_ref, gpre_ref, w1_ref, w2_ref, gpost_ref, o_ref, h_ref, acc_ref):
    j = pl.program_id(1)

    @pl.when(j == 0)
    def _():
        h_ref[...] = _rms(x_ref[...], gpre_ref[...]).astype(BF16)
        acc_ref[...] = jnp.zeros(acc_ref.shape, F32)

    a = jnp.square(jnp.maximum(_dot(h_ref[...], w1_ref[...]), 0.0))
    acc_ref[...] += _dot(a.astype(BF16), w2_ref[...])

    @pl.when(j == pl.num_programs(1) - 1)
    def _():
        o_ref[...] = x_ref[...] + _rms(acc_ref[...], gpost_ref[...])


def _mlp(x, g_pre, w1, w2, g_post, *, tm, tf):
    m, d = x.shape
    dff = w1.shape[1]
    return pl.pallas_call(
        _mlp_kernel,
        grid=(m // tm, dff // tf),
        in_specs=[
            pl.BlockSpec((tm, d), lambda i, j: (i, 0)),
            pl.BlockSpec((1, d), lambda i, j: (0, 0)),
            pl.BlockSpec((d, tf), lambda i, j: (0, j)),
            pl.BlockSpec((tf, d), lambda i, j: (j, 0)),
            pl.BlockSpec((1, d), lambda i, j: (0, 0)),
        ],
        out_specs=pl.BlockSpec((tm, d), lambda i, j: (i, 0)),
        out_shape=jax.ShapeDtypeStruct((m, d), F32),
        scratch_shapes=[pltpu.VMEM((tm, d), BF16), pltpu.VMEM((tm, d), F32)],
        compiler_params=_params("parallel", "arbitrary"),
        name="mlp",
    )(x, g_pre.reshape(1, d), w1, w2, g_post.reshape(1, d))


def _row_tile(m, cap):
    t = min(m, cap)
    assert m % t == 0
    return t


def _trunk_layer(x, w, lam_init, *, n_seq, seq, buf8, s0, mem, attend):
    m, d = x.shape
    proj, ab, k_rows, v_rows = _in_proj(x, w["g_pre_mix"], w["w_main"], w["w_ab"], tm=_row_tile(m, 1024))

    C = GDN_CHUNK
    seq_pad = -(-seq // C) * C
    if seq_pad == seq:
        xz, ab_p = proj, ab
    else:
        xz = jnp.pad(proj[:, :(COL_Z + 4) * LANES].reshape(n_seq, seq, -1), ((0, 0), (0, seq_pad - seq), (0, 0)))
        xz = xz.reshape(n_seq * seq_pad, -1)
        ab_p = jnp.pad(ab.reshape(n_seq, seq, -1), ((0, 0), (0, seq_pad - seq), (0, 0))).reshape(n_seq * seq_pad, -1)
    s0_arr, s0_offset = s0
    o_a, s_new = _gdn(xz, ab_p, buf8, s0_arr, s0_offset, w["w_conv"], w["gdn_a_log"], w["gdn_dt_bias"],
                      w["g_gdn_out"], n_seq=n_seq, seq_rows=seq_pad, valid_len=seq)
    if seq_pad != seq:
        o_a = o_a.reshape(n_seq, seq_pad, -1)[:, :seq].reshape(m, -1)

    o_b = attend(proj, k_rows, v_rows)
    mk_rows, mv_rows, mem_offset, n_mem = mem
    o_c = _cross_attn(proj, mk_rows, mv_rows, mem_offset, n_seq=n_seq, seq=seq, n_mem=n_mem,
                      tq=_row_tile(seq, 512))

    x = _merge(x, o_a, o_b, o_c, proj, w["w_branch"], w["w_out"], w["g_post_mix"], tm=_row_tile(m, 512))
    x = _mlp(x, w["g_pre_mlp"], w["w_ff1"], w["w_ff2"], w["g_post_mlp"], tm=_row_tile(m, 1024), tf=1024)
    return x, proj, k_rows, v_rows, s_new


def kernel(x_prompt, x_sample, mem_prompt, cache_diff_k, cache_diff_v, page_table, state_gdn, cache_gdn_conv,
           cache_mem_k, cache_mem_v, g_pre_mix, w_in, w_conv, gdn_a_log, gdn_dt_bias, g_gdn_out, diff_lambda,
           g_diff_sub, g_mem, w_mem_k, w_mem_v, w_branch, w_out, g_post_mix, g_pre_mlp, w_ff1, w_ff2, g_post_mlp):
    bp, sp, d = x_prompt.shape
    bs, ts, _ = x_sample.shape
    depth = w_in.shape[0]
    n_mem = mem_prompt.shape[1]
    qkv_w = 3 * GDN_HEADS * GDN_DK
    assert sp >= GDN_CONV - 1 and ts >= GDN_CONV - 1
    dk_w = DIFF_HEADS * 2 * DIFF_DQK

    xp = x_prompt.reshape(bp * sp, d)
    xs = x_sample.reshape(bs * ts, d)
    memf = mem_prompt.reshape(bp * n_mem, d)

    c_a = qkv_w + BRANCH_W
    c_dq = c_a + 2 * GDN_HEADS
    c_dk = c_dq + dk_w
    c_xq = c_dk + 2 * dk_w

    outs = {k: [] for k in ("kp", "vp", "sp", "cp", "mk", "mv", "ks", "vs", "ss", "cs")}
    zero_buf = jnp.zeros((bp, SUBLANES, qkv_w), F32)
    zero_state = jnp.zeros((bp, GDN_HEADS, GDN_DK, GDN_DK), F32)
    state_all = state_gdn.reshape(depth * bs, GDN_HEADS, GDN_DK, GDN_DK)
    mk_all = cache_mem_k.reshape(depth * bs * n_mem * XA_HEADS, XA_DH)
    mv_all = cache_mem_v.reshape(depth * bs * n_mem * XA_HEADS, XA_DH)
    for l in range(depth):
        lam_init = 0.8 - 0.6 * math.exp(-0.3 * l)
        w = {
            "g_pre_mix": g_pre_mix[l],
            "w_main": jnp.concatenate([w_in[l][:, :c_a], w_in[l][:, c_dq:c_dk], w_in[l][:, c_xq:],
                                       w_in[l][:, c_dk:c_xq]], axis=1).astype(BF16),
            "w_ab": w_in[l][:, c_a:c_dq].astype(BF16),
            "w_conv": w_conv[l], "gdn_a_log": gdn_a_log[l], "gdn_dt_bias": gdn_dt_bias[l],
            "g_gdn_out": g_gdn_out[l],
            "w_branch": w_branch[l].astype(BF16), "w_out": w_out[l].astype(BF16), "g_post_mix": g_post_mix[l],
            "g_pre_mlp": g_pre_mlp[l], "w_ff1": w_ff1[l].astype(BF16), "w_ff2": w_ff2[l].astype(BF16),
            "g_post_mlp": g_post_mlp[l],
        }
        lam_p = diff_lambda[l]
        g_sub = g_diff_sub[l]

        w_mem = jnp.concatenate([w_mem_k[l], w_mem_v[l]], axis=1).astype(BF16)
        mk_p, mv_p = _mem_proj(memf, g_mem[l], w_mem, tm=_row_tile(bp * n_mem, 1024))
        attend_p = functools.partial(_diff_prompt, lam_p=lam_p, g_sub=g_sub, n_seq=bp, seq=sp,
                                     tq=_row_tile(sp, 256), lam_init=lam_init)
        xp, proj_p, k_p, v_p, s_p = _trunk_layer(xp, w, lam_init, n_seq=bp, seq=sp, buf8=zero_buf,
                                                 s0=(zero_state, 0), mem=(mk_p, mv_p, 0, n_mem), attend=attend_p)
        proj3 = proj_p.reshape(bp, sp, -1)
        outs["kp"].append(k_p.reshape(bp, sp, DIFF_HEADS, -1))
        outs["vp"].append(v_p.reshape(bp, sp, DIFF_HEADS, -1))
        outs["sp"].append(s_p)
        outs["cp"].append(proj3[:, sp - (GDN_CONV - 1):, :qkv_w])
        outs["mk"].append(mk_p.reshape(bp, n_mem, XA_HEADS, XA_DH))
        outs["mv"].append(mv_p.reshape(bp, n_mem, XA_HEADS, XA_DH))

        buf8 = jnp.pad(cache_gdn_conv[l], ((0, 0), (SUBLANES - (GDN_CONV - 1), 0), (0, 0)))
        attend_s = functools.partial(_diff_paged, lam_p=lam_p, g_sub=g_sub, cache_k=cache_diff_k,
                                     cache_v=cache_diff_v, page_table=page_table, layer=l, t_new=ts,
                                     lam_init=lam_init)
        xs, proj_s, k_s, v_s, s_s = _trunk_layer(xs, w, lam_init, n_seq=bs, seq=ts, buf8=buf8,
                                                 s0=(state_all, l * bs), mem=(mk_all, mv_all, l * bs, n_mem),
                                                 attend=attend_s)
        proj3 = proj_s.reshape(bs, ts, -1)
        outs["ks"].append(k_s.reshape(bs, ts, DIFF_HEADS, -1))
        outs["vs"].append(v_s.reshape(bs, ts, DIFF_HEADS, -1))
        outs["ss"].append(s_s)
        outs["cs"].append(proj3[:, ts - (GDN_CONV - 1):, :qkv_w])

    st = lambda k: jnp.stack(outs[k])
    return (xp.reshape(bp, sp, d), xs.reshape(bs, ts, d),
            st("kp"), st("vp"), st("sp"), st("cp"), st("mk"), st("mv"),
            st("ks"), st("vs"), st("ss"), st("cs"))
```

```python
import functools
import math

import jax
import jax.numpy as jnp
from jax import lax
from jax.experimental import pallas as pl
from jax.experimental.pallas import tpu as pltpu

F32 = jnp.float32
BF16 = jnp.bfloat16
RMS_EPS = 1e-6
L2_EPS = 1e-6
LANES = 128
SUBLANES = 8
VMEM_LIMIT_BYTES = 48 * 1024 * 1024

GDN_HEADS = 4
GDN_DK = 128
GDN_CONV = 4
GDN_CHUNK = 128
GDN_SEQS_PER_STEP = 4
DIFF_HEADS = 4
DIFF_DQK = 64
XA_HEADS = 4
XA_DH = 128
N_BRANCH = 3
BRANCH_W = 512
PAGES_PER_STEP = 32
PAGE_GROUP = 4
LOG2_E = math.log2(math.e)

COL_QKV, COL_Z, COL_DQ, COL_XQ, COL_GATE = 0, 12, 16, 20, 24
MAIN_COLS = 48 * LANES


def _params(*sem):
    return pltpu.CompilerParams(dimension_semantics=sem, vmem_limit_bytes=VMEM_LIMIT_BYTES)


def _rms(x, g):
    return x * lax.rsqrt(jnp.mean(x * x, axis=-1, keepdims=True) + RMS_EPS) * g


def _sigmoid(x):
    return 1.0 / (1.0 + jnp.exp(-x))


def _dot(a, b, precision=None):
    return jnp.dot(a, b, preferred_element_type=F32, precision=precision)


def _dot_nt(a, b, precision=None):
    return lax.dot_general(a, b, (((1,), (1,)), ((), ())), preferred_element_type=F32, precision=precision)


def _dot_tn(a, b, precision=None):
    return lax.dot_general(a, b, (((0,), (0,)), ((), ())), preferred_element_type=F32, precision=precision)


def _unit_lower_inverses(lmats, eye, level, max_level):
    ts = [eye - jnp.where(level <= 1, lm, 0.0) for lm in lmats]
    for k in range(2, max_level + 1):
        t16s = [t.astype(BF16) for t in ts]
        xs = [_dot(jnp.where(level == k, lm, 0.0).astype(BF16), t16) for lm, t16 in zip(lmats, t16s)]
        ts = [t - _dot(t16, x.astype(BF16)) for t, t16, x in zip(ts, t16s, xs)]
    return ts


def _diff_lambda(lam_ref, lam_init):
    lp = lam_ref[...]
    a = jnp.sum(lp[0:1] * lp[1:2], axis=-1, keepdims=True)
    b = jnp.sum(lp[2:3] * lp[3:4], axis=-1, keepdims=True)
    return jnp.exp(a) - jnp.exp(b) + lam_init


def _mem_proj_kernel(x_ref, g_ref, w_ref, k_ref, v_ref):
    tm = x_ref.shape[0]
    res = _dot(_rms(x_ref[...], g_ref[...]).astype(BF16), w_ref[...])
    for hd in range(XA_HEADS):
        k_ref[pl.ds(hd, tm, stride=XA_HEADS), :] = res[:, hd * XA_DH:(hd + 1) * XA_DH]
        v_ref[pl.ds(hd, tm, stride=XA_HEADS), :] = res[:, (XA_HEADS + hd) * XA_DH:(XA_HEADS + hd + 1) * XA_DH]


def _mem_proj(x, g, w, *, tm):
    m, d = x.shape
    n = w.shape[1]
    rows = jax.ShapeDtypeStruct((m * XA_HEADS, XA_DH), F32)
    return pl.pallas_call(
        _mem_proj_kernel,
        grid=(m // tm,),
        in_specs=[pl.BlockSpec((tm, d), lambda i: (i, 0)),
                  pl.BlockSpec((1, d), lambda i: (0, 0)),
                  pl.BlockSpec((d, n), lambda i: (0, 0))],
        out_specs=[pl.BlockSpec((tm * XA_HEADS, XA_DH), lambda i: (i, 0)),
                   pl.BlockSpec((tm * XA_HEADS, XA_DH), lambda i: (i, 0))],
        out_shape=[rows, rows],
        compiler_params=_params("parallel"),
        name="mem_proj",
    )(x, g.reshape(1, d), w)


def _in_proj_kernel(x_ref, g_ref, w_ref, ws_ref, o_ref, os_ref, k_ref, v_ref, h_ref, *, n_main):
    j = pl.program_id(1)
    tm = x_ref.shape[0]
    kw = 2 * DIFF_DQK

    @pl.when(j == 0)
    def _():
        h = _rms(x_ref[...], g_ref[...]).astype(BF16)
        h_ref[...] = h
        os_ref[...] = _dot(h, ws_ref[...])

    tn = o_ref.shape[1]
    res = _dot(h_ref[...], w_ref[:, pl.ds(pl.multiple_of(j * tn, tn), tn)])

    @pl.when(j < n_main)
    def _():
        o_ref[...] = res

    @pl.when(j == n_main)
    def _():
        for hd in range(DIFF_HEADS):
            k_ref[pl.ds(hd, tm, stride=DIFF_HEADS), :] = res[:, hd * kw:(hd + 1) * kw]
            v_ref[pl.ds(hd, tm, stride=DIFF_HEADS), :] = res[:, (DIFF_HEADS + hd) * kw:(DIFF_HEADS + hd + 1) * kw]


def _in_proj(x, g, w, w_side, *, tm):
    m, d = x.shape
    tn = 2 * DIFF_HEADS * 2 * DIFF_DQK
    n_main = MAIN_COLS // tn
    assert w.shape[1] == MAIN_COLS + tn and MAIN_COLS % tn == 0
    ns = w_side.shape[1]
    kw = 2 * DIFF_DQK
    return pl.pallas_call(
        functools.partial(_in_proj_kernel, n_main=n_main),
        grid=(m // tm, n_main + 1),
        in_specs=[pl.BlockSpec((tm, d), lambda i, j: (i, 0)),
                  pl.BlockSpec((1, d), lambda i, j: (0, 0)),
                  pl.BlockSpec((d, MAIN_COLS + tn), lambda i, j: (0, 0), pipeline_mode=pl.Buffered(1)),
                  pl.BlockSpec((d, ns), lambda i, j: (0, 0))],
        out_specs=[pl.BlockSpec((tm, tn), lambda i, j: (i, jnp.minimum(j, n_main - 1))),
                   pl.BlockSpec((tm, ns), lambda i, j: (i, 0)),
                   pl.BlockSpec((tm * DIFF_HEADS, kw), lambda i, j: (i, 0)),
                   pl.BlockSpec((tm * DIFF_HEADS, kw), lambda i, j: (i, 0))],
        out_shape=[jax.ShapeDtypeStruct((m, MAIN_COLS), F32),
                   jax.ShapeDtypeStruct((m, ns), F32),
                   jax.ShapeDtypeStruct((m * DIFF_HEADS, kw), F32),
                   jax.ShapeDtypeStruct((m * DIFF_HEADS, kw), F32)],
        scratch_shapes=[pltpu.VMEM((tm, d), BF16)],
        compiler_params=_params("parallel", "arbitrary"),
        name="in_proj",
    )(x, g.reshape(1, d), w, w_side)


def _gdn_kernel(qkv_ref, ab_ref, z_ref, buf_ref, s0_ref, wconv_ref, alog_ref, dtb_ref, gout_ref,
                o_ref, sfin_ref, ext_ref, s_ref, prep_ref, *, chunk, n_chunks, valid_len):
    C = chunk
    G, rows = qkv_ref.shape[0], qkv_ref.shape[1]
    H = GDN_HEADS
    DK = GDN_DK
    c = pl.program_id(1)
    hi = lax.Precision.HIGHEST
    live = rows if valid_len is None else min(rows, -(-valid_len // SUBLANES) * SUBLANES)
    assert rows == C or n_chunks == 1

    @pl.when(c == 0)
    def _():
        ext_ref[:, 0:SUBLANES, :] = buf_ref[...]
        s_ref[...] = s0_ref[...]

    row = lax.broadcasted_iota(jnp.int32, (C, C), 0)
    col = lax.broadcasted_iota(jnp.int32, (C, C), 1)
    incl = row >= col
    strict = row > col

    def prepare_block(g, blk):
        first = SUBLANES - (GDN_CONV - 1)
        sl = slice(blk * DK, (blk + 1) * DK)
        yb = ext_ref[g, first:first + live, sl] * wconv_ref[0:1, sl]
        for i in range(1, GDN_CONV):
            yb = yb + ext_ref[g, first + i:first + i + live, sl] * wconv_ref[i:i + 1, sl]
        yb = yb * _sigmoid(yb)
        if blk < 2 * H:
            yb = yb * lax.rsqrt(jnp.sum(yb * yb, axis=-1, keepdims=True) + L2_EPS)
        if blk < H:
            yb = yb * (DK ** -0.5)
        prep_ref[g, 0:live, sl] = yb
        if live < C:
            prep_ref[g, live:C, sl] = jnp.zeros((C - live, DK), F32)

    def prepare_gates(g):
        ab = ab_ref[g]
        sp_in = ab + dtb_ref[...]
        softplus = jnp.maximum(sp_in, 0.0) + jnp.log(1.0 + jnp.exp(-jnp.abs(sp_in)))
        g_all = -jnp.exp(alog_ref[...]) * softplus
        beta_all = _sigmoid(ab)
        if valid_len is not None:
            row_ok = (c * C + lax.broadcasted_iota(jnp.int32, (rows, 1), 0)) < valid_len
            g_all = jnp.where(row_ok, g_all, 0.0)
            beta_all = jnp.where(row_ok, beta_all, 0.0)
        if rows < C:
            pad = jnp.zeros((C - rows, 2 * H), F32)
            g_all = jnp.concatenate([g_all, pad], axis=0)
            beta_all = jnp.concatenate([beta_all, pad], axis=0)
        tril = jnp.where(incl, 1.0, 0.0).astype(F32)
        return _dot(tril, g_all, hi), beta_all

    def recur(gates):
        eye = jnp.where(row == col, 1.0, 0.0).astype(F32)
        level = 32 - lax.clz(row ^ col)
        pairs = [(g, h) for g in range(G) for h in range(H)]
        idx = range(len(pairs))
        q = [prep_ref[g, :, h * DK:(h + 1) * DK] for g, h in pairs]
        k = [prep_ref[g, :, (H + h) * DK:(H + h + 1) * DK] for g, h in pairs]
        v = [prep_ref[g, :, (2 * H + h) * DK:(2 * H + h + 1) * DK] for g, h in pairs]
        beta = [gates[g][1][:, H + h:H + h + 1] for g, h in pairs]
        g_cum = [gates[g][0][:, h:h + 1] for g, h in pairs]
        decay = []
        for p in idx:
            g_cum_b = jnp.broadcast_to(g_cum[p], (C, C))
            decay.append(jnp.where(incl, jnp.exp(jnp.where(incl, g_cum_b - g_cum_b.T, 0.0)), 0.0))
        kb = [k[p] * beta[p] for p in idx]
        k16 = [x.astype(BF16) for x in k]
        lmat = [jnp.where(strict, _dot_nt(kb[p].astype(BF16), k16[p]) * decay[p], 0.0) for p in idx]
        qk = [(_dot_nt(q[p].astype(BF16), k16[p]) * decay[p]).astype(BF16) for p in idx]
        e_g = [jnp.exp(g_cum[p]) for p in idx]
        rhs = [jnp.concatenate([v[p] * beta[p], kb[p] * e_g[p]], axis=1).astype(BF16) for p in idx]
        g_last = [g_cum[p][C - 1:C] for p in idx]
        qg = [(q[p] * e_g[p]).astype(BF16) for p in idx]
        kg = [(k[p] * jnp.exp(g_last[p] - g_cum[p])).astype(BF16) for p in idx]
        tinv = _unit_lower_inverses(lmat, eye, level, max_level=(live - 1).bit_length())
        sol = [_dot(tinv[p].astype(BF16), rhs[p]) for p in idx]
        s = [s_ref[g, h] for g, h in pairs]
        s16 = [x.astype(BF16) for x in s]
        u = [sol[p][:, :DK] - _dot(sol[p][:, DK:].astype(BF16), s16[p]) for p in idx]
        u16 = [x.astype(BF16) for x in u]
        o = [_dot(qg[p], s16[p]) + _dot(qk[p], u16[p]) for p in idx]
        s_new = [s[p] * jnp.exp(g_last[p]) + _dot_tn(kg[p], u16[p]) for p in idx]
        for p, (g, h) in enumerate(pairs):
            s_ref[g, h] = s_new[p]
            zh = z_ref[g, :, h * DK:(h + 1) * DK]
            o_ref[g, :, h * DK:(h + 1) * DK] = _rms(o[p][:rows], gout_ref[...]) * (zh * _sigmoid(zh))

    ext_ref[:, SUBLANES:SUBLANES + rows, :] = qkv_ref[...]
    for g in range(G):
        for blk in range(3 * H):
            prepare_block(g, blk)
    recur([prepare_gates(g) for g in range(G)])
    if n_chunks > 1:
        ext_ref[:, 0:SUBLANES, :] = ext_ref[:, C:C + SUBLANES, :]

    @pl.when(c == pl.num_programs(1) - 1)
    def _():
        sfin_ref[...] = s_ref[...]


def _gdn(xz, ab, buf8, s0, s0_offset, w_conv, a_log, dt_bias, g_out, *, n_seq, seq_rows, valid_len):
    C = GDN_CHUNK
    if seq_rows % C == 0:
        rows, nc = C, seq_rows // C
    else:
        assert seq_rows < C and seq_rows % SUBLANES == 0
        rows, nc = seq_rows, 1
    G = GDN_SEQS_PER_STEP if (n_seq % GDN_SEQS_PER_STEP == 0 and s0_offset % GDN_SEQS_PER_STEP == 0) else 1
    s0_blk = s0_offset // G
    qkv_w = 3 * GDN_HEADS * GDN_DK
    zeros_h = jnp.zeros((GDN_HEADS,), F32)
    alog8 = jnp.concatenate([a_log, zeros_h]).reshape(1, 2 * GDN_HEADS)
    dtb8 = jnp.concatenate([dt_bias, zeros_h]).reshape(1, 2 * GDN_HEADS)
    state = (G, GDN_HEADS, GDN_DK, GDN_DK)
    xz3 = xz.reshape(n_seq, seq_rows, xz.shape[1])
    ab3 = ab.reshape(n_seq, seq_rows, 2 * GDN_HEADS)
    o_a, s_new = pl.pallas_call(
        functools.partial(_gdn_kernel, chunk=C, n_chunks=nc,
                          valid_len=None if valid_len == seq_rows else valid_len),
        grid=(n_seq // G, nc),
        in_specs=[
            pl.BlockSpec((G, rows, qkv_w), lambda b, c: (b, c, 0)),
            pl.BlockSpec((G, rows, 2 * GDN_HEADS), lambda b, c: (b, c, 0)),
            pl.BlockSpec((G, rows, BRANCH_W), lambda b, c: (b, c, COL_Z * LANES // BRANCH_W)),
            pl.BlockSpec((G, SUBLANES, qkv_w), lambda b, c: (b, 0, 0)),
            pl.BlockSpec(state, lambda b, c: (s0_blk + b, 0, 0, 0)),
            pl.BlockSpec((GDN_CONV, qkv_w), lambda b, c: (0, 0)),
            pl.BlockSpec((1, 2 * GDN_HEADS), lambda b, c: (0, 0)),
            pl.BlockSpec((1, 2 * GDN_HEADS), lambda b, c: (0, 0)),
            pl.BlockSpec((1, GDN_DK), lambda b, c: (0, 0)),
        ],
        out_specs=[
            pl.BlockSpec((G, rows, BRANCH_W), lambda b, c: (b, c, 0)),
            pl.BlockSpec(state, lambda b, c: (b, 0, 0, 0)),
        ],
        out_shape=[jax.ShapeDtypeStruct((n_seq, seq_rows, BRANCH_W), F32),
                   jax.ShapeDtypeStruct((n_seq, GDN_HEADS, GDN_DK, GDN_DK), F32)],
        scratch_shapes=[pltpu.VMEM((G, SUBLANES + C, qkv_w), F32),
                        pltpu.VMEM(state, F32),
                        pltpu.VMEM((G, C, qkv_w), F32)],
        compiler_params=_params("parallel", "arbitrary"),
        name="gdn",
    )(xz3, ab3, xz3, buf8, s0, w_conv, alog8, dtb8, g_out.reshape(1, GDN_DK))
    return o_a.reshape(n_seq * seq_rows, BRANCH_W), s_new


def _split_components(q):
    lane = lax.broadcasted_iota(jnp.int32, q.shape, 1)
    return jnp.concatenate([jnp.where(lane < DIFF_DQK, q, 0.0), jnp.where(lane >= DIFF_DQK, q, 0.0)], axis=0)


def _diff_prompt_kernel(lam_ref, q_ref, k_ref, v_ref, g_ref, o_ref,
                        k16_ref, v16_ref, qbd_ref, m_ref, l_ref, acc_ref, *, tq, lam_init):
    i = pl.program_id(1)
    dv = 2 * DIFF_DQK

    @pl.when(i == 0)
    def _():
        seq = k16_ref.shape[1]
        for h in range(DIFF_HEADS):
            k16_ref[h] = k_ref[pl.ds(h, seq, stride=DIFF_HEADS), :].astype(BF16)
            v16_ref[h] = v_ref[pl.ds(h, seq, stride=DIFF_HEADS), :].astype(BF16)

    q = q_ref[...] * (DIFF_DQK ** -0.5 * LOG2_E)
    for h in range(DIFF_HEADS):
        qbd_ref[h] = _split_components(q[:, h * dv:(h + 1) * dv]).astype(BF16)
    m_ref[...] = jnp.full(m_ref.shape, -jnp.inf, F32)
    l_ref[...] = jnp.zeros(l_ref.shape, F32)
    acc_ref[...] = jnp.zeros(acc_ref.shape, F32)

    def step(j, masked):
        start = pl.multiple_of(j * tq, tq)
        for h in range(DIFF_HEADS):
            k = k16_ref[h, pl.ds(start, tq), :]
            v = v16_ref[h, pl.ds(start, tq), :]
            s = _dot_nt(qbd_ref[h], k)
            if masked:
                r = lax.broadcasted_iota(jnp.int32, (tq, tq), 0)
                cidx = lax.broadcasted_iota(jnp.int32, (tq, tq), 1)
                keep = jnp.concatenate([cidx <= r, cidx <= r], axis=0)
                s = jnp.where(keep, s, -jnp.inf)
            m_old = m_ref[h]
            m_new = jnp.maximum(m_old, jnp.max(s, axis=-1, keepdims=True))
            alpha = jnp.exp2(m_old - m_new)
            p = jnp.exp2(s - jnp.concatenate([m_new] * (tq // dv), axis=1))
            l_ref[h] = l_ref[h] * alpha + jnp.sum(p, axis=-1, keepdims=True)
            acc_ref[h] = acc_ref[h] * alpha + _dot(p.astype(BF16), v)
            m_ref[h] = m_new

    def body(j, carry):
        step(j, False)
        return carry

    lax.fori_loop(0, i, body, 0)
    step(i, True)
    lam = _diff_lambda(lam_ref, lam_init)
    for h in range(DIFF_HEADS):
        o = acc_ref[h] / l_ref[h]
        a = o[:tq] - lam * o[tq:]
        o_ref[:, h * dv:(h + 1) * dv] = _rms(a, g_ref[...]) * (1.0 - lam_init)


def _diff_prompt(proj, k_rows, v_rows, lam_p, g_sub, *, n_seq, seq, tq, lam_init):
    nq = seq // tq
    dv = 2 * DIFF_DQK
    w = DIFF_HEADS * dv
    stat = pltpu.VMEM((DIFF_HEADS, 2 * tq, dv), F32)
    kv16 = pltpu.VMEM((DIFF_HEADS, seq, dv), BF16)
    return pl.pallas_call(
        functools.partial(_diff_prompt_kernel, tq=tq, lam_init=lam_init),
        grid=(n_seq, nq),
        in_specs=[
            pl.BlockSpec((4, DIFF_DQK), lambda b, i: (0, 0)),
            pl.BlockSpec((tq, w), lambda b, i: (b * nq + i, COL_DQ * LANES // w)),
            pl.BlockSpec((seq * DIFF_HEADS, dv), lambda b, i: (b, 0)),
            pl.BlockSpec((seq * DIFF_HEADS, dv), lambda b, i: (b, 0)),
            pl.BlockSpec((1, dv), lambda b, i: (0, 0)),
        ],
        out_specs=pl.BlockSpec((tq, w), lambda b, i: (b * nq + i, 0)),
        out_shape=jax.ShapeDtypeStruct((n_seq * seq, w), F32),
        scratch_shapes=[kv16, kv16, pltpu.VMEM((DIFF_HEADS, 2 * tq, dv), BF16), stat, stat, stat],
        compiler_params=_params("arbitrary", "arbitrary"),
        name="diff_attn_prompt",
    )(lam_p, proj, k_rows, v_rows, g_sub.reshape(1, dv))


def _diff_paged_kernel(pt_ref, lam_ref, q_ref, ks_ref, vs_ref, g_ref, *rest, t_new, n_pg, group, lam_init):
    del pt_ref
    k_pages = rest[:n_pg]
    v_pages = rest[n_pg:2 * n_pg]
    o_ref, qall_ref, m_ref, l_ref, acc_ref = rest[2 * n_pg:]
    T = t_new
    R = 2 * T
    HR = DIFF_HEADS * R
    dv = 2 * DIFF_DQK
    prow = k_pages[0].shape[0]
    step_idx = pl.program_id(1)

    @pl.when(step_idx == 0)
    def _():
        q = q_ref[...] * (DIFF_DQK ** -0.5)
        t_of_row = lax.broadcasted_iota(jnp.int32, (R, 1), 0) % T
        for h in range(DIFF_HEADS):
            sl = slice(h * dv, (h + 1) * dv)
            rows = slice(h * R, (h + 1) * R)
            qbd = _split_components(q[:, sl])
            qall_ref[rows, :] = qbd.astype(BF16)
            ks = ks_ref[pl.ds(h, T, stride=DIFF_HEADS), :]
            vs = vs_ref[pl.ds(h, T, stride=DIFF_HEADS), :]
            scores = []
            for j in range(T):
                sj = jnp.sum(qbd * ks[j:j + 1], axis=-1, keepdims=True)
                scores.append(jnp.where(t_of_row >= j, sj, -jnp.inf))
            m = scores[0]
            for j in range(1, T):
                m = jnp.maximum(m, scores[j])
            den = jnp.zeros((R, 1), F32)
            acc = jnp.zeros((R, dv), F32)
            for j in range(T):
                pj = jnp.exp(scores[j] - m)
                den = den + pj
                acc = acc + pj * vs[j:j + 1]
            m_ref[rows, :] = jnp.broadcast_to(m, (R, dv))
            l_ref[rows, :] = jnp.broadcast_to(den, (R, dv))
            acc_ref[rows, :] = acc

    qall = qall_ref[...]
    row_head = lax.broadcasted_iota(jnp.int32, (HR, prow), 0) // R
    col_head = lax.broadcasted_iota(jnp.int32, (HR, prow), 1) % DIFF_HEADS
    bias = jnp.where(row_head == col_head, 0.0, -jnp.inf)

    parts = []
    for g0 in range(0, n_pg, group):
        tiles = [_dot_nt(qall, k_pages[n][...].astype(BF16)) + bias for n in range(g0, g0 + group)]
        mx = tiles[0]
        for t in tiles[1:]:
            mx = jnp.maximum(mx, t)
        m_g = jnp.max(mx, axis=-1, keepdims=True)
        l_g = None
        acc_g = None
        for n, t in zip(range(g0, g0 + group), tiles):
            p = jnp.exp(t - m_g)
            ps = jnp.sum(p, axis=-1, keepdims=True)
            pv = _dot(p.astype(BF16), v_pages[n][...].astype(BF16))
            l_g = ps if l_g is None else l_g + ps
            acc_g = pv if acc_g is None else acc_g + pv
        parts.append((m_g, l_g, acc_g))

    m_old = m_ref[...]
    m_new = m_old
    for m_g, _, _ in parts:
        m_new = jnp.maximum(m_new, m_g)
    a_old = jnp.exp(m_old - m_new)
    l_new = l_ref[...] * a_old
    acc_new = acc_ref[...] * a_old
    for m_g, l_g, acc_g in parts:
        wgt = jnp.exp(m_g - m_new)
        l_new = l_new + l_g * wgt
        acc_new = acc_new + acc_g * wgt
    m_ref[...] = m_new
    l_ref[...] = l_new
    acc_ref[...] = acc_new

    @pl.when(step_idx == pl.num_programs(1) - 1)
    def _():
        lam = _diff_lambda(lam_ref, lam_init)
        o = acc_ref[...] / l_ref[...]
        for h in range(DIFF_HEADS):
            a = o[h * R:h * R + T] - lam * o[h * R + T:(h + 1) * R]
            o_ref[:, h * dv:(h + 1) * dv] = _rms(a, g_ref[...]) * (1.0 - lam_init)


def _diff_paged(proj, k_rows, v_rows, lam_p, g_sub, cache_k, cache_v, page_table, *, layer, t_new, lam_init):
    depth, n_pool, page, heads, kw = cache_k.shape
    n_seq, n_pages = page_table.shape
    assert heads == DIFF_HEADS and kw == 2 * DIFF_DQK and cache_v.shape == cache_k.shape
    n_pg = min(PAGES_PER_STEP, n_pages)
    group = min(PAGE_GROUP, n_pg)
    assert n_pages % n_pg == 0 and n_pg % group == 0
    n_steps = n_pages // n_pg
    dv = 2 * DIFF_DQK
    w = heads * kw
    ck = cache_k.reshape(depth * n_pool, page * heads, kw)
    cv = cache_v.reshape(depth * n_pool, page * heads, kw)
    pt = page_table.reshape(-1)
    base = layer * n_pool

    def page_spec(n):
        return pl.BlockSpec((None, page * heads, kw),
                            lambda b, s, pt: (base + pt[b * n_pages + s * n_pg + n], 0, 0))

    stat = pltpu.VMEM((DIFF_HEADS * 2 * t_new, dv), F32)
    grid_spec = pltpu.PrefetchScalarGridSpec(
        num_scalar_prefetch=1,
        grid=(n_seq, n_steps),
        in_specs=[
            pl.BlockSpec((4, DIFF_DQK), lambda b, s, pt: (0, 0)),
            pl.BlockSpec((t_new, w), lambda b, s, pt: (b, COL_DQ * LANES // w)),
            pl.BlockSpec((t_new * heads, kw), lambda b, s, pt: (b, 0)),
            pl.BlockSpec((t_new * heads, kw), lambda b, s, pt: (b, 0)),
            pl.BlockSpec((1, dv), lambda b, s, pt: (0, 0)),
        ] + [page_spec(n) for n in range(n_pg)] + [page_spec(n) for n in range(n_pg)],
        out_specs=pl.BlockSpec((t_new, w), lambda b, s, pt: (b, 0)),
        scratch_shapes=[pltpu.VMEM((DIFF_HEADS * 2 * t_new, dv), BF16), stat, stat, stat],
    )
    return pl.pallas_call(
        functools.partial(_diff_paged_kernel, t_new=t_new, n_pg=n_pg, group=group, lam_init=lam_init),
        grid_spec=grid_spec,
        out_shape=jax.ShapeDtypeStruct((n_seq * t_new, w), F32),
        compiler_params=_params("parallel", "arbitrary"),
        name="diff_attn_paged",
    )(pt, lam_p, proj, k_rows, v_rows, g_sub.reshape(1, dv), *([ck] * n_pg), *([cv] * n_pg))


def _cross_kernel(q_ref, mk_ref, mv_ref, o_ref):
    q = q_ref[...] * (XA_DH ** -0.5)
    n_mem = mk_ref.shape[0] // XA_HEADS
    for h in range(XA_HEADS):
        sl = slice(h * XA_DH, (h + 1) * XA_DH)
        mk = mk_ref[pl.ds(h, n_mem, stride=XA_HEADS), :].astype(BF16)
        mv = mv_ref[pl.ds(h, n_mem, stride=XA_HEADS), :].astype(BF16)
        s = _dot_nt(q[:, sl].astype(BF16), mk)
        p = jnp.exp(s - jnp.max(s, axis=-1, keepdims=True))
        den = jnp.sum(p, axis=-1, keepdims=True)
        o_ref[:, sl] = _dot(p.astype(BF16), mv) / den


def _cross_attn(proj, mk_rows, mv_rows, seq_offset, *, n_seq, seq, n_mem, tq):
    nq = seq // tq
    w = XA_HEADS * XA_DH
    return pl.pallas_call(
        _cross_kernel,
        grid=(n_seq, nq),
        in_specs=[
            pl.BlockSpec((tq, w), lambda b, i: (b * nq + i, COL_XQ * LANES // w)),
            pl.BlockSpec((n_mem * XA_HEADS, XA_DH), lambda b, i: (seq_offset + b, 0)),
            pl.BlockSpec((n_mem * XA_HEADS, XA_DH), lambda b, i: (seq_offset + b, 0)),
        ],
        out_specs=pl.BlockSpec((tq, w), lambda b, i: (b * nq + i, 0)),
        out_shape=jax.ShapeDtypeStruct((n_seq * seq, w), F32),
        compiler_params=_params("parallel", "arbitrary"),
        name="cross_attn",
    )(proj, mk_rows, mv_rows)


def _merge_kernel(x_ref, oa_ref, ob_ref, oc_ref, ga_ref, gb_ref, gc_ref, wb_ref, wo_ref, g_ref, o_ref):
    merged = None
    for n, (br, gl) in enumerate(((oa_ref, ga_ref), (ob_ref, gb_ref), (oc_ref, gc_ref))):
        up = _dot(br[...].astype(BF16), wb_ref[n])
        term = _sigmoid(gl[...]) * up
        merged = term if merged is None else merged + term
    y = _dot(merged.astype(BF16), wo_ref[...])
    o_ref[...] = x_ref[...] + _rms(y, g_ref[...])


def _merge(x, oa, ob, oc, proj, w_branch, w_out, g_post, *, tm):
    m, d = x.shape
    gate0 = COL_GATE * LANES // d
    row = lambda i: (i, 0)
    return pl.pallas_call(
        _merge_kernel,
        grid=(m // tm,),
        in_specs=[
            pl.BlockSpec((tm, d), row),
            pl.BlockSpec((tm, BRANCH_W), row), pl.BlockSpec((tm, BRANCH_W), row), pl.BlockSpec((tm, BRANCH_W), row),
            pl.BlockSpec((tm, d), lambda i: (i, gate0)),
            pl.BlockSpec((tm, d), lambda i: (i, gate0 + 1)),
            pl.BlockSpec((tm, d), lambda i: (i, gate0 + 2)),
            pl.BlockSpec((N_BRANCH, BRANCH_W, d), lambda i: (0, 0, 0)),
            pl.BlockSpec((d, d), lambda i: (0, 0)),
            pl.BlockSpec((1, d), lambda i: (0, 0)),
        ],
        out_specs=pl.BlockSpec((tm, d), row),
        out_shape=jax.ShapeDtypeStruct((m, d), F32),
        compiler_params=_params("parallel"),
        name="merge",
    )(x, oa, ob, oc, proj, proj, proj, w_branch, w_out, g_post.reshape(1, d))


def _mlp_kernel(x_ref, gpre_ref, w1_ref, w2_ref, gpost_ref, o_ref, h_ref, acc_ref, *, tf):
    j = pl.program_id(1)

    @pl.when(j == 0)
    def _():
        h_ref[...] = _rms(x_ref[...], gpre_ref[...]).astype(BF16)
        acc_ref[...] = jnp.zeros(acc_ref.shape, F32)

    cols = pl.ds(pl.multiple_of(j * tf, tf), tf)
    a = jnp.square(jnp.maximum(_dot(h_ref[...], w1_ref[:, cols]), 0.0))
    acc_ref[...] += _dot(a.astype(BF16), w2_ref[cols, :])

    @pl.when(j == pl.num_programs(1) - 1)
    def _():
        o_ref[...] = x_ref[...] + _rms(acc_ref[...], gpost_ref[...])


def _mlp(x, g_pre, w1, w2, g_post, *, tm, tf):
    m, d = x.shape
    dff = w1.shape[1]
    return pl.pallas_call(
        functools.partial(_mlp_kernel, tf=tf),
        grid=(m // tm, dff // tf),
        in_specs=[
            pl.BlockSpec((tm, d), lambda i, j: (i, 0)),
            pl.BlockSpec((1, d), lambda i, j: (0, 0)),
            pl.BlockSpec((d, dff), lambda i, j: (0, 0), pipeline_mode=pl.Buffered(1)),
            pl.BlockSpec((dff, d), lambda i, j: (0, 0), pipeline_mode=pl.Buffered(1)),
            pl.BlockSpec((1, d), lambda i, j: (0, 0)),
        ],
        out_specs=pl.BlockSpec((tm, d), lambda i, j: (i, 0)),
        out_shape=jax.ShapeDtypeStruct((m, d), F32),
        scratch_shapes=[pltpu.VMEM((tm, d), BF16), pltpu.VMEM((tm, d), F32)],
        compiler_params=_params("parallel", "arbitrary"),
        name="mlp",
    )(x, g_pre.reshape(1, d), w1, w2, g_post.reshape(1, d))


def _row_tile(m, cap):
    t = min(m, cap)
    assert m % t == 0
    return t


def _trunk_layer(x, w, lam_init, *, n_seq, seq, buf8, s0, mem, attend):
    m, d = x.shape
    proj, ab, k_rows, v_rows = _in_proj(x, w["g_pre_mix"], w["w_main"], w["w_ab"], tm=_row_tile(m, 1024))

    C = GDN_CHUNK
    short = seq < C and seq % SUBLANES == 0
    seq_pad = seq if short else -(-seq // C) * C
    if seq_pad == seq:
        xz, ab_p = proj, ab
    else:
        xz = jnp.pad(proj[:, :(COL_Z + 4) * LANES].reshape(n_seq, seq, -1), ((0, 0), (0, seq_pad - seq), (0, 0)))
        xz = xz.reshape(n_seq * seq_pad, -1)
        ab_p = jnp.pad(ab.reshape(n_seq, seq, -1), ((0, 0), (0, seq_pad - seq), (0, 0))).reshape(n_seq * seq_pad, -1)
    s0_arr, s0_offset = s0
    o_a, s_new = _gdn(xz, ab_p, buf8, s0_arr, s0_offset, w["w_conv"], w["gdn_a_log"], w["gdn_dt_bias"],
                      w["g_gdn_out"], n_seq=n_seq, seq_rows=seq_pad, valid_len=seq)
    if seq_pad != seq:
        o_a = o_a.reshape(n_seq, seq_pad, -1)[:, :seq].reshape(m, -1)

    o_b = attend(proj, k_rows, v_rows)
    mk_rows, mv_rows, mem_offset, n_mem = mem
    o_c = _cross_attn(proj, mk_rows, mv_rows, mem_offset, n_seq=n_seq, seq=seq, n_mem=n_mem,
                      tq=_row_tile(seq, 512))

    x = _merge(x, o_a, o_b, o_c, proj, w["w_branch"], w["w_out"], w["g_post_mix"], tm=_row_tile(m, 512))
    x = _mlp(x, w["g_pre_mlp"], w["w_ff1"], w["w_ff2"], w["g_post_mlp"], tm=_row_tile(m, 1024), tf=1024)
    return x, proj, k_rows, v_rows, s_new


def kernel(x_prompt, x_sample, mem_prompt, cache_diff_k, cache_diff_v, page_table, state_gdn, cache_gdn_conv,
           cache_mem_k, cache_mem_v, g_pre_mix, w_in, w_conv, gdn_a_log, gdn_dt_bias, g_gdn_out, diff_lambda,
           g_diff_sub, g_mem, w_mem_k, w_mem_v, w_branch, w_out, g_post_mix, g_pre_mlp, w_ff1, w_ff2, g_post_mlp):
    bp, sp, d = x_prompt.shape
    bs, ts, _ = x_sample.shape
    depth = w_in.shape[0]
    n_mem = mem_prompt.shape[1]
    qkv_w = 3 * GDN_HEADS * GDN_DK
    assert sp >= GDN_CONV - 1 and ts >= GDN_CONV - 1
    dk_w = DIFF_HEADS * 2 * DIFF_DQK

    xp = x_prompt.reshape(bp * sp, d)
    xs = x_sample.reshape(bs * ts, d)
    memf = mem_prompt.reshape(bp * n_mem, d)

    c_a = qkv_w + BRANCH_W
    c_dq = c_a + 2 * GDN_HEADS
    c_dk = c_dq + dk_w
    c_xq = c_dk + 2 * dk_w

    outs = {k: [] for k in ("kp", "vp", "sp", "cp", "mk", "mv", "ks", "vs", "ss", "cs")}
    zero_buf = jnp.zeros((bp, SUBLANES, qkv_w), F32)
    zero_state = jnp.zeros((bp, GDN_HEADS, GDN_DK, GDN_DK), F32)
    state_all = state_gdn.reshape(depth * bs, GDN_HEADS, GDN_DK, GDN_DK)
    mk_all = cache_mem_k.reshape(depth * bs * n_mem * XA_HEADS, XA_DH)
    mv_all = cache_mem_v.reshape(depth * bs * n_mem * XA_HEADS, XA_DH)
    for l in range(depth):
        lam_init = 0.8 - 0.6 * math.exp(-0.3 * l)
        w = {
            "g_pre_mix": g_pre_mix[l],
            "w_main": jnp.concatenate([w_in[l][:, :c_a], w_in[l][:, c_dq:c_dk], w_in[l][:, c_xq:],
                                       w_in[l][:, c_dk:c_xq]], axis=1).astype(BF16),
            "w_ab": w_in[l][:, c_a:c_dq].astype(BF16),
            "w_conv": w_conv[l], "gdn_a_log": gdn_a_log[l], "gdn_dt_bias": gdn_dt_bias[l],
            "g_gdn_out": g_gdn_out[l],
            "w_branch": w_branch[l].astype(BF16), "w_out": w_out[l].astype(BF16), "g_post_mix": g_post_mix[l],
            "g_pre_mlp": g_pre_mlp[l], "w_ff1": w_ff1[l].astype(BF16), "w_ff2": w_ff2[l].astype(BF16),
            "g_post_mlp": g_post_mlp[l],
        }
        lam_p = diff_lambda[l]
        g_sub = g_diff_sub[l]

        w_mem = jnp.concatenate([w_mem_k[l], w_mem_v[l]], axis=1).astype(BF16)
        mk_p, mv_p = _mem_proj(memf, g_mem[l], w_mem, tm=_row_tile(bp * n_mem, 1024))
        attend_p = functools.partial(_diff_prompt, lam_p=lam_p, g_sub=g_sub, n_seq=bp, seq=sp,
                                     tq=_row_tile(sp, 256), lam_init=lam_init)
        xp, proj_p, k_p, v_p, s_p = _trunk_layer(xp, w, lam_init, n_seq=bp, seq=sp, buf8=zero_buf,
                                                 s0=(zero_state, 0), mem=(mk_p, mv_p, 0, n_mem), attend=attend_p)
        proj3 = proj_p.reshape(bp, sp, -1)
        outs["kp"].append(k_p.reshape(bp, sp, DIFF_HEADS, -1))
        outs["vp"].append(v_p.reshape(bp, sp, DIFF_HEADS, -1))
        outs["sp"].append(s_p)
        outs["cp"].append(proj3[:, sp - (GDN_CONV - 1):, :qkv_w])
        outs["mk"].append(mk_p.reshape(bp, n_mem, XA_HEADS, XA_DH))
        outs["mv"].append(mv_p.reshape(bp, n_mem, XA_HEADS, XA_DH))

        buf8 = jnp.pad(cache_gdn_conv[l], ((0, 0), (SUBLANES - (GDN_CONV - 1), 0), (0, 0)))
        attend_s = functools.partial(_diff_paged, lam_p=lam_p, g_sub=g_sub, cache_k=cache_diff_k,
                                     cache_v=cache_diff_v, page_table=page_table, layer=l, t_new=ts,
                                     lam_init=lam_init)
        xs, proj_s, k_s, v_s, s_s = _trunk_layer(xs, w, lam_init, n_seq=bs, seq=ts, buf8=buf8,
                                                 s0=(state_all, l * bs), mem=(mk_all, mv_all, l * bs, n_mem),
                                                 attend=attend_s)
        proj3 = proj_s.reshape(bs, ts, -1)
        outs["ks"].append(k_s.reshape(bs, ts, DIFF_HEADS, -1))
        outs["vs"].append(v_s.reshape(bs, ts, DIFF_HEADS, -1))
        outs["ss"].append(s_s)
        outs["cs"].append(proj3[:, ts - (GDN_CONV - 1):, :qkv_w])

    st = lambda k: jnp.stack(outs[k])
    return (xp.reshape(bp, sp, d), xs.reshape(bs, ts, d),
            st("kp"), st("vp"), st("sp"), st("cp"), st("mk"), st("mv"),
            st("ks"), st("vs"), st("ss"), st("cs"))
```

```python
import functools
import math

import jax
import jax.numpy as jnp
from jax import lax
from jax.experimental import pallas as pl
from jax.experimental.pallas import tpu as pltpu

F32 = jnp.float32
BF16 = jnp.bfloat16
RMS_EPS = 1e-6
L2_EPS = 1e-6
LANES = 128
SUBLANES = 8
VMEM_LIMIT_BYTES = 48 * 1024 * 1024

GDN_HEADS = 4
GDN_DK = 128
GDN_CONV = 4
GDN_CHUNK = 128
GDN_SEQS_PER_STEP = 4
DIFF_HEADS = 4
DIFF_DQK = 64
XA_HEADS = 4
XA_DH = 128
N_BRANCH = 3
BRANCH_W = 512
PAGES_PER_STEP = 32
PAGE_GROUP = 4
LOG2_E = math.log2(math.e)

COL_QKV, COL_Z, COL_DQ, COL_XQ, COL_GATE = 0, 12, 16, 20, 24
MAIN_COLS = 48 * LANES


def _params(*sem):
    return pltpu.CompilerParams(dimension_semantics=sem, vmem_limit_bytes=VMEM_LIMIT_BYTES)


def _rms(x, g):
    return x * lax.rsqrt(jnp.mean(x * x, axis=-1, keepdims=True) + RMS_EPS) * g


def _sigmoid(x):
    return 1.0 / (1.0 + jnp.exp(-x))


def _dot(a, b, precision=None):
    return jnp.dot(a, b, preferred_element_type=F32, precision=precision)


def _dot_nt(a, b, precision=None):
    return lax.dot_general(a, b, (((1,), (1,)), ((), ())), preferred_element_type=F32, precision=precision)


def _dot_tn(a, b, precision=None):
    return lax.dot_general(a, b, (((0,), (0,)), ((), ())), preferred_element_type=F32, precision=precision)


def _unit_lower_inverses(lmats, eye, level, max_level):
    ts = [eye - jnp.where(level <= 1, lm, 0.0) for lm in lmats]
    for k in range(2, max_level + 1):
        t16s = [t.astype(BF16) for t in ts]
        xs = [_dot(jnp.where(level == k, lm, 0.0).astype(BF16), t16) for lm, t16 in zip(lmats, t16s)]
        ts = [t - _dot(t16, x.astype(BF16)) for t, t16, x in zip(ts, t16s, xs)]
    return ts


def _diff_lambda(lam_ref, lam_init):
    lp = lam_ref[...]
    a = jnp.sum(lp[0:1] * lp[1:2], axis=-1, keepdims=True)
    b = jnp.sum(lp[2:3] * lp[3:4], axis=-1, keepdims=True)
    return jnp.exp(a) - jnp.exp(b) + lam_init


def _mem_proj_kernel(x_ref, g_ref, w_ref, k_ref, v_ref):
    tm = x_ref.shape[0]
    res = _dot(_rms(x_ref[...], g_ref[...]).astype(BF16), w_ref[...])
    for hd in range(XA_HEADS):
        k_ref[pl.ds(hd, tm, stride=XA_HEADS), :] = res[:, hd * XA_DH:(hd + 1) * XA_DH]
        v_ref[pl.ds(hd, tm, stride=XA_HEADS), :] = res[:, (XA_HEADS + hd) * XA_DH:(XA_HEADS + hd + 1) * XA_DH]


def _mem_proj(x, g, w, *, tm):
    m, d = x.shape
    n = w.shape[1]
    rows = jax.ShapeDtypeStruct((m * XA_HEADS, XA_DH), F32)
    return pl.pallas_call(
        _mem_proj_kernel,
        grid=(m // tm,),
        in_specs=[pl.BlockSpec((tm, d), lambda i: (i, 0)),
                  pl.BlockSpec((1, d), lambda i: (0, 0)),
                  pl.BlockSpec((d, n), lambda i: (0, 0))],
        out_specs=[pl.BlockSpec((tm * XA_HEADS, XA_DH), lambda i: (i, 0)),
                   pl.BlockSpec((tm * XA_HEADS, XA_DH), lambda i: (i, 0))],
        out_shape=[rows, rows],
        compiler_params=_params("parallel"),
        name="mem_proj",
    )(x, g.reshape(1, d), w)


def _in_proj_kernel(x_ref, g_ref, w_ref, ws_ref, o_ref, os_ref, k_ref, v_ref, h_ref, *, n_main):
    j = pl.program_id(1)
    tm = x_ref.shape[0]
    kw = 2 * DIFF_DQK

    @pl.when(j == 0)
    def _():
        h = _rms(x_ref[...], g_ref[...]).astype(BF16)
        h_ref[...] = h
        os_ref[...] = _dot(h, ws_ref[...])

    tn = o_ref.shape[1]
    res = _dot(h_ref[...], w_ref[:, pl.ds(pl.multiple_of(j * tn, tn), tn)])

    @pl.when(j < n_main)
    def _():
        o_ref[...] = res

    @pl.when(j == n_main)
    def _():
        for hd in range(DIFF_HEADS):
            k_ref[pl.ds(hd, tm, stride=DIFF_HEADS), :] = res[:, hd * kw:(hd + 1) * kw]
            v_ref[pl.ds(hd, tm, stride=DIFF_HEADS), :] = res[:, (DIFF_HEADS + hd) * kw:(DIFF_HEADS + hd + 1) * kw]


def _in_proj(x, g, w, w_side, *, tm):
    m, d = x.shape
    tn = 2 * DIFF_HEADS * 2 * DIFF_DQK
    n_main = MAIN_COLS // tn
    assert w.shape[1] == MAIN_COLS + tn and MAIN_COLS % tn == 0
    ns = w_side.shape[1]
    kw = 2 * DIFF_DQK
    return pl.pallas_call(
        functools.partial(_in_proj_kernel, n_main=n_main),
        grid=(m // tm, n_main + 1),
        in_specs=[pl.BlockSpec((tm, d), lambda i, j: (i, 0)),
                  pl.BlockSpec((1, d), lambda i, j: (0, 0)),
                  pl.BlockSpec((d, MAIN_COLS + tn), lambda i, j: (0, 0), pipeline_mode=pl.Buffered(1)),
                  pl.BlockSpec((d, ns), lambda i, j: (0, 0))],
        out_specs=[pl.BlockSpec((tm, tn), lambda i, j: (i, jnp.minimum(j, n_main - 1))),
                   pl.BlockSpec((tm, ns), lambda i, j: (i, 0)),
                   pl.BlockSpec((tm * DIFF_HEADS, kw), lambda i, j: (i, 0)),
                   pl.BlockSpec((tm * DIFF_HEADS, kw), lambda i, j: (i, 0))],
        out_shape=[jax.ShapeDtypeStruct((m, MAIN_COLS), F32),
                   jax.ShapeDtypeStruct((m, ns), F32),
                   jax.ShapeDtypeStruct((m * DIFF_HEADS, kw), F32),
                   jax.ShapeDtypeStruct((m * DIFF_HEADS, kw), F32)],
        scratch_shapes=[pltpu.VMEM((tm, d), BF16)],
        compiler_params=_params("parallel", "arbitrary"),
        name="in_proj",
    )(x, g.reshape(1, d), w, w_side)


def _gdn_kernel(qkv_ref, ab_ref, z_ref, buf_ref, s0_ref, wconv_ref, alog_ref, dtb_ref, gout_ref,
                o_ref, sfin_ref, ext_ref, s_ref, prep_ref, *, chunk, n_chunks, valid_len):
    C = chunk
    G, rows = qkv_ref.shape[0], qkv_ref.shape[1]
    H = GDN_HEADS
    DK = GDN_DK
    c = pl.program_id(1)
    hi = lax.Precision.HIGHEST
    live = rows if valid_len is None else min(rows, -(-valid_len // SUBLANES) * SUBLANES)
    assert rows == C or n_chunks == 1

    @pl.when(c == 0)
    def _():
        ext_ref[:, 0:SUBLANES, :] = buf_ref[...]
        s_ref[...] = s0_ref[...]

    row = lax.broadcasted_iota(jnp.int32, (C, C), 0)
    col = lax.broadcasted_iota(jnp.int32, (C, C), 1)
    incl = row >= col
    strict = row > col

    def prepare_block(g, blk):
        first = SUBLANES - (GDN_CONV - 1)
        sl = slice(blk * DK, (blk + 1) * DK)
        yb = ext_ref[g, first:first + live, sl] * wconv_ref[0:1, sl]
        for i in range(1, GDN_CONV):
            yb = yb + ext_ref[g, first + i:first + i + live, sl] * wconv_ref[i:i + 1, sl]
        yb = yb * _sigmoid(yb)
        if blk < 2 * H:
            yb = yb * lax.rsqrt(jnp.sum(yb * yb, axis=-1, keepdims=True) + L2_EPS)
        if blk < H:
            yb = yb * (DK ** -0.5)
        prep_ref[g, 0:live, sl] = yb
        if live < C:
            prep_ref[g, live:C, sl] = jnp.zeros((C - live, DK), F32)

    def prepare_gates(g):
        ab = ab_ref[g]
        sp_in = ab + dtb_ref[...]
        softplus = jnp.maximum(sp_in, 0.0) + jnp.log(1.0 + jnp.exp(-jnp.abs(sp_in)))
        g_all = -jnp.exp(alog_ref[...]) * softplus
        beta_all = _sigmoid(ab)
        if valid_len is not None:
            row_ok = (c * C + lax.broadcasted_iota(jnp.int32, (rows, 1), 0)) < valid_len
            g_all = jnp.where(row_ok, g_all, 0.0)
            beta_all = jnp.where(row_ok, beta_all, 0.0)
        if rows < C:
            pad = jnp.zeros((C - rows, 2 * H), F32)
            g_all = jnp.concatenate([g_all, pad], axis=0)
            beta_all = jnp.concatenate([beta_all, pad], axis=0)
        tril = jnp.where(incl, 1.0, 0.0).astype(F32)
        return _dot(tril, g_all, hi), beta_all

    def recur(gates):
        eye = jnp.where(row == col, 1.0, 0.0).astype(F32)
        level = 32 - lax.clz(row ^ col)
        pairs = [(g, h) for g in range(G) for h in range(H)]
        idx = range(len(pairs))
        q = [prep_ref[g, :, h * DK:(h + 1) * DK] for g, h in pairs]
        k = [prep_ref[g, :, (H + h) * DK:(H + h + 1) * DK] for g, h in pairs]
        v = [prep_ref[g, :, (2 * H + h) * DK:(2 * H + h + 1) * DK] for g, h in pairs]
        beta = [gates[g][1][:, H + h:H + h + 1] for g, h in pairs]
        g_cum = [gates[g][0][:, h:h + 1] for g, h in pairs]
        decay = []
        for p in idx:
            g_cum_b = jnp.broadcast_to(g_cum[p], (C, C))
            decay.append(jnp.where(incl, jnp.exp(jnp.where(incl, g_cum_b - g_cum_b.T, 0.0)), 0.0))
        kb = [k[p] * beta[p] for p in idx]
        k16 = [x.astype(BF16) for x in k]
        lmat = [jnp.where(strict, _dot_nt(kb[p].astype(BF16), k16[p]) * decay[p], 0.0) for p in idx]
        qk = [(_dot_nt(q[p].astype(BF16), k16[p]) * decay[p]).astype(BF16) for p in idx]
        e_g = [jnp.exp(g_cum[p]) for p in idx]
        rhs = [jnp.concatenate([v[p] * beta[p], kb[p] * e_g[p]], axis=1).astype(BF16) for p in idx]
        g_last = [g_cum[p][C - 1:C] for p in idx]
        qg = [(q[p] * e_g[p]).astype(BF16) for p in idx]
        kg = [(k[p] * jnp.exp(g_last[p] - g_cum[p])).astype(BF16) for p in idx]
        tinv = _unit_lower_inverses(lmat, eye, level, max_level=(live - 1).bit_length())
        sol = [_dot(tinv[p].astype(BF16), rhs[p]) for p in idx]
        s = [s_ref[g, h] for g, h in pairs]
        s16 = [x.astype(BF16) for x in s]
        u = [sol[p][:, :DK] - _dot(sol[p][:, DK:].astype(BF16), s16[p]) for p in idx]
        u16 = [x.astype(BF16) for x in u]
        o = [_dot(qg[p], s16[p]) + _dot(qk[p], u16[p]) for p in idx]
        s_new = [s[p] * jnp.exp(g_last[p]) + _dot_tn(kg[p], u16[p]) for p in idx]
        for p, (g, h) in enumerate(pairs):
            s_ref[g, h] = s_new[p]
            zh = z_ref[g, :, h * DK:(h + 1) * DK]
            o_ref[g, :, h * DK:(h + 1) * DK] = _rms(o[p][:rows], gout_ref[...]) * (zh * _sigmoid(zh))

    ext_ref[:, SUBLANES:SUBLANES + rows, :] = qkv_ref[...]
    for g in range(G):
        for blk in range(3 * H):
            prepare_block(g, blk)
    recur([prepare_gates(g) for g in range(G)])
    if n_chunks > 1:
        ext_ref[:, 0:SUBLANES, :] = ext_ref[:, C:C + SUBLANES, :]

    @pl.when(c == pl.num_programs(1) - 1)
    def _():
        sfin_ref[...] = s_ref[...]


def _gdn(xz, ab, buf8, s0, s0_offset, w_conv, a_log, dt_bias, g_out, *, n_seq, seq_rows, valid_len):
    C = GDN_CHUNK
    if seq_rows % C == 0:
        rows, nc = C, seq_rows // C
    else:
        assert seq_rows < C and seq_rows % SUBLANES == 0
        rows, nc = seq_rows, 1
    G = GDN_SEQS_PER_STEP if (n_seq % GDN_SEQS_PER_STEP == 0 and s0_offset % GDN_SEQS_PER_STEP == 0) else 1
    s0_blk = s0_offset // G
    qkv_w = 3 * GDN_HEADS * GDN_DK
    zeros_h = jnp.zeros((GDN_HEADS,), F32)
    alog8 = jnp.concatenate([a_log, zeros_h]).reshape(1, 2 * GDN_HEADS)
    dtb8 = jnp.concatenate([dt_bias, zeros_h]).reshape(1, 2 * GDN_HEADS)
    state = (G, GDN_HEADS, GDN_DK, GDN_DK)
    xz3 = xz.reshape(n_seq, seq_rows, xz.shape[1])
    ab3 = ab.reshape(n_seq, seq_rows, 2 * GDN_HEADS)
    o_a, s_new = pl.pallas_call(
        functools.partial(_gdn_kernel, chunk=C, n_chunks=nc,
                          valid_len=None if valid_len == seq_rows else valid_len),
        grid=(n_seq // G, nc),
        in_specs=[
            pl.BlockSpec((G, rows, qkv_w), lambda b, c: (b, c, 0)),
            pl.BlockSpec((G, rows, 2 * GDN_HEADS), lambda b, c: (b, c, 0)),
            pl.BlockSpec((G, rows, BRANCH_W), lambda b, c: (b, c, COL_Z * LANES // BRANCH_W)),
            pl.BlockSpec((G, SUBLANES, qkv_w), lambda b, c: (b, 0, 0)),
            pl.BlockSpec(state, lambda b, c: (s0_blk + b, 0, 0, 0)),
            pl.BlockSpec((GDN_CONV, qkv_w), lambda b, c: (0, 0)),
            pl.BlockSpec((1, 2 * GDN_HEADS), lambda b, c: (0, 0)),
            pl.BlockSpec((1, 2 * GDN_HEADS), lambda b, c: (0, 0)),
            pl.BlockSpec((1, GDN_DK), lambda b, c: (0, 0)),
        ],
        out_specs=[
            pl.BlockSpec((G, rows, BRANCH_W), lambda b, c: (b, c, 0)),
            pl.BlockSpec(state, lambda b, c: (b, 0, 0, 0)),
        ],
        out_shape=[jax.ShapeDtypeStruct((n_seq, seq_rows, BRANCH_W), F32),
                   jax.ShapeDtypeStruct((n_seq, GDN_HEADS, GDN_DK, GDN_DK), F32)],
        scratch_shapes=[pltpu.VMEM((G, SUBLANES + C, qkv_w), F32),
                        pltpu.VMEM(state, F32),
                        pltpu.VMEM((G, C, qkv_w), F32)],
        compiler_params=_params("parallel", "arbitrary"),
        name="gdn",
    )(xz3, ab3, xz3, buf8, s0, w_conv, alog8, dtb8, g_out.reshape(1, GDN_DK))
    return o_a.reshape(n_seq * seq_rows, BRANCH_W), s_new


def _split_components(q):
    lane = lax.broadcasted_iota(jnp.int32, q.shape, 1)
    return jnp.concatenate([jnp.where(lane < DIFF_DQK, q, 0.0), jnp.where(lane >= DIFF_DQK, q, 0.0)], axis=0)


def _diff_prompt_kernel(lam_ref, q_ref, k_ref, v_ref, g_ref, o_ref,
                        k16_ref, v16_ref, qbd_ref, m_ref, l_ref, acc_ref, *, tq, lam_init):
    i = pl.program_id(1)
    dv = 2 * DIFF_DQK

    @pl.when(i == 0)
    def _():
        seq = k16_ref.shape[1]
        for h in range(DIFF_HEADS):
            k16_ref[h] = k_ref[pl.ds(h, seq, stride=DIFF_HEADS), :].astype(BF16)
            v16_ref[h] = v_ref[pl.ds(h, seq, stride=DIFF_HEADS), :].astype(BF16)

    q = q_ref[...] * (DIFF_DQK ** -0.5 * LOG2_E)
    for h in range(DIFF_HEADS):
        qbd_ref[h] = _split_components(q[:, h * dv:(h + 1) * dv]).astype(BF16)
    m_ref[...] = jnp.full(m_ref.shape, -jnp.inf, F32)
    l_ref[...] = jnp.zeros(l_ref.shape, F32)
    acc_ref[...] = jnp.zeros(acc_ref.shape, F32)

    def step(start, width, masked):
        for h in range(DIFF_HEADS):
            k = k16_ref[h, pl.ds(start, width), :]
            v = v16_ref[h, pl.ds(start, width), :]
            s = _dot_nt(qbd_ref[h], k)
            if masked:
                r = lax.broadcasted_iota(jnp.int32, (tq, tq), 0)
                cidx = lax.broadcasted_iota(jnp.int32, (tq, tq), 1)
                keep = jnp.concatenate([cidx <= r, cidx <= r], axis=0)
                s = jnp.where(keep, s, -jnp.inf)
            m_old = m_ref[h]
            m_new = jnp.maximum(m_old, jnp.max(s, axis=-1, keepdims=True))
            alpha = jnp.exp2(m_old - m_new)
            p = jnp.exp2(s - jnp.concatenate([m_new] * (width // dv), axis=1))
            l_ref[h] = l_ref[h] * alpha + jnp.sum(p, axis=-1, keepdims=True)
            acc_ref[h] = acc_ref[h] * alpha + _dot(p.astype(BF16), v)
            m_ref[h] = m_new

    def body(j, carry):
        step(pl.multiple_of(j * (2 * tq), 2 * tq), 2 * tq, False)
        return carry

    lax.fori_loop(0, lax.shift_right_logical(i, 1), body, 0)

    @pl.when((i & 1) == 1)
    def _():
        step(pl.multiple_of((i - 1) * tq, tq), tq, False)

    step(pl.multiple_of(i * tq, tq), tq, True)
    lam = _diff_lambda(lam_ref, lam_init)
    for h in range(DIFF_HEADS):
        o = acc_ref[h] / l_ref[h]
        a = o[:tq] - lam * o[tq:]
        o_ref[:, h * dv:(h + 1) * dv] = _rms(a, g_ref[...]) * (1.0 - lam_init)


def _diff_prompt(proj, k_rows, v_rows, lam_p, g_sub, *, n_seq, seq, tq, lam_init):
    nq = seq // tq
    dv = 2 * DIFF_DQK
    w = DIFF_HEADS * dv
    stat = pltpu.VMEM((DIFF_HEADS, 2 * tq, dv), F32)
    kv16 = pltpu.VMEM((DIFF_HEADS, seq, dv), BF16)
    return pl.pallas_call(
        functools.partial(_diff_prompt_kernel, tq=tq, lam_init=lam_init),
        grid=(n_seq, nq),
        in_specs=[
            pl.BlockSpec((4, DIFF_DQK), lambda b, i: (0, 0)),
            pl.BlockSpec((tq, w), lambda b, i: (b * nq + i, COL_DQ * LANES // w)),
            pl.BlockSpec((seq * DIFF_HEADS, dv), lambda b, i: (b, 0)),
            pl.BlockSpec((seq * DIFF_HEADS, dv), lambda b, i: (b, 0)),
            pl.BlockSpec((1, dv), lambda b, i: (0, 0)),
        ],
        out_specs=pl.BlockSpec((tq, w), lambda b, i: (b * nq + i, 0)),
        out_shape=jax.ShapeDtypeStruct((n_seq * seq, w), F32),
        scratch_shapes=[kv16, kv16, pltpu.VMEM((DIFF_HEADS, 2 * tq, dv), BF16), stat, stat, stat],
        compiler_params=_params("arbitrary", "arbitrary"),
        name="diff_attn_prompt",
    )(lam_p, proj, k_rows, v_rows, g_sub.reshape(1, dv))


def _diff_paged_kernel(pt_ref, lam_ref, q_ref, ks_ref, vs_ref, g_ref, *rest, t_new, n_pg, group, lam_init):
    del pt_ref
    k_pages = rest[:n_pg]
    v_pages = rest[n_pg:2 * n_pg]
    o_ref, qall_ref, m_ref, l_ref, acc_ref = rest[2 * n_pg:]
    T = t_new
    R = 2 * T
    HR = DIFF_HEADS * R
    dv = 2 * DIFF_DQK
    prow = k_pages[0].shape[0]
    step_idx = pl.program_id(1)

    @pl.when(step_idx == 0)
    def _():
        q = q_ref[...] * (DIFF_DQK ** -0.5)
        t_of_row = lax.broadcasted_iota(jnp.int32, (R, 1), 0) % T
        for h in range(DIFF_HEADS):
            sl = slice(h * dv, (h + 1) * dv)
            rows = slice(h * R, (h + 1) * R)
            qbd = _split_components(q[:, sl])
            qall_ref[rows, :] = qbd.astype(BF16)
            ks = ks_ref[pl.ds(h, T, stride=DIFF_HEADS), :]
            vs = vs_ref[pl.ds(h, T, stride=DIFF_HEADS), :]
            scores = []
            for j in range(T):
                sj = jnp.sum(qbd * ks[j:j + 1], axis=-1, keepdims=True)
                scores.append(jnp.where(t_of_row >= j, sj, -jnp.inf))
            m = scores[0]
            for j in range(1, T):
                m = jnp.maximum(m, scores[j])
            den = jnp.zeros((R, 1), F32)
            acc = jnp.zeros((R, dv), F32)
            for j in range(T):
                pj = jnp.exp(scores[j] - m)
                den = den + pj
                acc = acc + pj * vs[j:j + 1]
            m_ref[rows, :] = jnp.broadcast_to(m, (R, dv))
            l_ref[rows, :] = jnp.broadcast_to(den, (R, dv))
            acc_ref[rows, :] = acc

    qall = qall_ref[...]
    row_head = lax.broadcasted_iota(jnp.int32, (HR, prow), 0) // R
    col_head = lax.broadcasted_iota(jnp.int32, (HR, prow), 1) % DIFF_HEADS
    bias = jnp.where(row_head == col_head, 0.0, -jnp.inf)

    parts = []
    for g0 in range(0, n_pg, group):
        tiles = [_dot_nt(qall, k_pages[n][...].astype(BF16)) + bias for n in range(g0, g0 + group)]
        mx = tiles[0]
        for t in tiles[1:]:
            mx = jnp.maximum(mx, t)
        m_g = jnp.max(mx, axis=-1, keepdims=True)
        l_g = None
        acc_g = None
        for n, t in zip(range(g0, g0 + group), tiles):
            p = jnp.exp(t - m_g)
            ps = jnp.sum(p, axis=-1, keepdims=True)
            pv = _dot(p.astype(BF16), v_pages[n][...].astype(BF16))
            l_g = ps if l_g is None else l_g + ps
            acc_g = pv if acc_g is None else acc_g + pv
        parts.append((m_g, l_g, acc_g))

    m_old = m_ref[...]
    m_new = m_old
    for m_g, _, _ in parts:
        m_new = jnp.maximum(m_new, m_g)
    a_old = jnp.exp(m_old - m_new)
    l_new = l_ref[...] * a_old
    acc_new = acc_ref[...] * a_old
    for m_g, l_g, acc_g in parts:
        wgt = jnp.exp(m_g - m_new)
        l_new = l_new + l_g * wgt
        acc_new = acc_new + acc_g * wgt
    m_ref[...] = m_new
    l_ref[...] = l_new
    acc_ref[...] = acc_new

    @pl.when(step_idx == pl.num_programs(1) - 1)
    def _():
        lam = _diff_lambda(lam_ref, lam_init)
        o = acc_ref[...] / l_ref[...]
        for h in range(DIFF_HEADS):
            a = o[h * R:h * R + T] - lam * o[h * R + T:(h + 1) * R]
            o_ref[:, h * dv:(h + 1) * dv] = _rms(a, g_ref[...]) * (1.0 - lam_init)


def _diff_paged(proj, k_rows, v_rows, lam_p, g_sub, cache_k, cache_v, page_table, *, layer, t_new, lam_init):
    depth, n_pool, page, heads, kw = cache_k.shape
    n_seq, n_pages = page_table.shape
    assert heads == DIFF_HEADS and kw == 2 * DIFF_DQK and cache_v.shape == cache_k.shape
    n_pg = min(PAGES_PER_STEP, n_pages)
    group = min(PAGE_GROUP, n_pg)
    assert n_pages % n_pg == 0 and n_pg % group == 0
    n_steps = n_pages // n_pg
    dv = 2 * DIFF_DQK
    w = heads * kw
    ck = cache_k.reshape(depth * n_pool, page * heads, kw)
    cv = cache_v.reshape(depth * n_pool, page * heads, kw)
    pt = page_table.reshape(-1)
    base = layer * n_pool

    def page_spec(n):
        return pl.BlockSpec((None, page * heads, kw),
                            lambda b, s, pt: (base + pt[b * n_pages + s * n_pg + n], 0, 0))

    stat = pltpu.VMEM((DIFF_HEADS * 2 * t_new, dv), F32)
    grid_spec = pltpu.PrefetchScalarGridSpec(
        num_scalar_prefetch=1,
        grid=(n_seq, n_steps),
        in_specs=[
            pl.BlockSpec((4, DIFF_DQK), lambda b, s, pt: (0, 0)),
            pl.BlockSpec((t_new, w), lambda b, s, pt: (b, COL_DQ * LANES // w)),
            pl.BlockSpec((t_new * heads, kw), lambda b, s, pt: (b, 0)),
            pl.BlockSpec((t_new * heads, kw), lambda b, s, pt: (b, 0)),
            pl.BlockSpec((1, dv), lambda b, s, pt: (0, 0)),
        ] + [page_spec(n) for n in range(n_pg)] + [page_spec(n) for n in range(n_pg)],
        out_specs=pl.BlockSpec((t_new, w), lambda b, s, pt: (b, 0)),
        scratch_shapes=[pltpu.VMEM((DIFF_HEADS * 2 * t_new, dv), BF16), stat, stat, stat],
    )
    return pl.pallas_call(
        functools.partial(_diff_paged_kernel, t_new=t_new, n_pg=n_pg, group=group, lam_init=lam_init),
        grid_spec=grid_spec,
        out_shape=jax.ShapeDtypeStruct((n_seq * t_new, w), F32),
        compiler_params=_params("parallel", "arbitrary"),
        name="diff_attn_paged",
    )(pt, lam_p, proj, k_rows, v_rows, g_sub.reshape(1, dv), *([ck] * n_pg), *([cv] * n_pg))


def _cross_kernel(q_ref, mk_ref, mv_ref, o_ref):
    q = q_ref[...] * (XA_DH ** -0.5)
    n_mem = mk_ref.shape[0] // XA_HEADS
    for h in range(XA_HEADS):
        sl = slice(h * XA_DH, (h + 1) * XA_DH)
        mk = mk_ref[pl.ds(h, n_mem, stride=XA_HEADS), :].astype(BF16)
        mv = mv_ref[pl.ds(h, n_mem, stride=XA_HEADS), :].astype(BF16)
        s = _dot_nt(q[:, sl].astype(BF16), mk)
        p = jnp.exp(s - jnp.max(s, axis=-1, keepdims=True))
        den = jnp.sum(p, axis=-1, keepdims=True)
        o_ref[:, sl] = _dot(p.astype(BF16), mv) / den


def _cross_attn(proj, mk_rows, mv_rows, seq_offset, *, n_seq, seq, n_mem, tq):
    nq = seq // tq
    w = XA_HEADS * XA_DH
    return pl.pallas_call(
        _cross_kernel,
        grid=(n_seq, nq),
        in_specs=[
            pl.BlockSpec((tq, w), lambda b, i: (b * nq + i, COL_XQ * LANES // w)),
            pl.BlockSpec((n_mem * XA_HEADS, XA_DH), lambda b, i: (seq_offset + b, 0)),
            pl.BlockSpec((n_mem * XA_HEADS, XA_DH), lambda b, i: (seq_offset + b, 0)),
        ],
        out_specs=pl.BlockSpec((tq, w), lambda b, i: (b * nq + i, 0)),
        out_shape=jax.ShapeDtypeStruct((n_seq * seq, w), F32),
        compiler_params=_params("parallel", "arbitrary"),
        name="cross_attn",
    )(proj, mk_rows, mv_rows)


def _merge_kernel(x_ref, oa_ref, ob_ref, oc_ref, ga_ref, gb_ref, gc_ref, wb_ref, wo_ref, g_ref, o_ref):
    merged = None
    for n, (br, gl) in enumerate(((oa_ref, ga_ref), (ob_ref, gb_ref), (oc_ref, gc_ref))):
        up = _dot(br[...].astype(BF16), wb_ref[n])
        term = _sigmoid(gl[...]) * up
        merged = term if merged is None else merged + term
    y = _dot(merged.astype(BF16), wo_ref[...])
    o_ref[...] = x_ref[...] + _rms(y, g_ref[...])


def _merge(x, oa, ob, oc, proj, w_branch, w_out, g_post, *, tm):
    m, d = x.shape
    gate0 = COL_GATE * LANES // d
    row = lambda i: (i, 0)
    return pl.pallas_call(
        _merge_kernel,
        grid=(m // tm,),
        in_specs=[
            pl.BlockSpec((tm, d), row),
            pl.BlockSpec((tm, BRANCH_W), row), pl.BlockSpec((tm, BRANCH_W), row), pl.BlockSpec((tm, BRANCH_W), row),
            pl.BlockSpec((tm, d), lambda i: (i, gate0)),
            pl.BlockSpec((tm, d), lambda i: (i, gate0 + 1)),
            pl.BlockSpec((tm, d), lambda i: (i, gate0 + 2)),
            pl.BlockSpec((N_BRANCH, BRANCH_W, d), lambda i: (0, 0, 0)),
            pl.BlockSpec((d, d), lambda i: (0, 0)),
            pl.BlockSpec((1, d), lambda i: (0, 0)),
        ],
        out_specs=pl.BlockSpec((tm, d), row),
        out_shape=jax.ShapeDtypeStruct((m, d), F32),
        compiler_params=_params("parallel"),
        name="merge",
    )(x, oa, ob, oc, proj, proj, proj, w_branch, w_out, g_post.reshape(1, d))


def _mlp_kernel(x_ref, gpre_ref, w1_ref, w2_ref, gpost_ref, o_ref, h_ref, acc_ref, *, tf):
    j = pl.program_id(1)

    @pl.when(j == 0)
    def _():
        h_ref[...] = _rms(x_ref[...], gpre_ref[...]).astype(BF16)
        acc_ref[...] = jnp.zeros(acc_ref.shape, F32)

    cols = pl.ds(pl.multiple_of(j * tf, tf), tf)
    a = jnp.square(jnp.maximum(_dot(h_ref[...], w1_ref[:, cols]), 0.0))
    acc_ref[...] += _dot(a.astype(BF16), w2_ref[cols, :])

    @pl.when(j == pl.num_programs(1) - 1)
    def _():
        o_ref[...] = x_ref[...] + _rms(acc_ref[...], gpost_ref[...])


def _mlp(x, g_pre, w1, w2, g_post, *, tm, tf):
    m, d = x.shape
    dff = w1.shape[1]
    return pl.pallas_call(
        functools.partial(_mlp_kernel, tf=tf),
        grid=(m // tm, dff // tf),
        in_specs=[
            pl.BlockSpec((tm, d), lambda i, j: (i, 0)),
            pl.BlockSpec((1, d), lambda i, j: (0, 0)),
            pl.BlockSpec((d, dff), lambda i, j: (0, 0), pipeline_mode=pl.Buffered(1)),
            pl.BlockSpec((dff, d), lambda i, j: (0, 0), pipeline_mode=pl.Buffered(1)),
            pl.BlockSpec((1, d), lambda i, j: (0, 0)),
        ],
        out_specs=pl.BlockSpec((tm, d), lambda i, j: (i, 0)),
        out_shape=jax.ShapeDtypeStruct((m, d), F32),
        scratch_shapes=[pltpu.VMEM((tm, d), BF16), pltpu.VMEM((tm, d), F32)],
        compiler_params=_params("parallel", "arbitrary"),
        name="mlp",
    )(x, g_pre.reshape(1, d), w1, w2, g_post.reshape(1, d))


def _row_tile(m, cap):
    t = min(m, cap)
    assert m % t == 0
    return t


def _trunk_layer(x, w, lam_init, *, n_seq, seq, buf8, s0, mem, attend):
    m, d = x.shape
    proj, ab, k_rows, v_rows = _in_proj(x, w["g_pre_mix"], w["w_main"], w["w_ab"], tm=_row_tile(m, 1024))

    C = GDN_CHUNK
    short = seq < C and seq % SUBLANES == 0
    seq_pad = seq if short else -(-seq // C) * C
    if seq_pad == seq:
        xz, ab_p = proj, ab
    else:
        xz = jnp.pad(proj[:, :(COL_Z + 4) * LANES].reshape(n_seq, seq, -1), ((0, 0), (0, seq_pad - seq), (0, 0)))
        xz = xz.reshape(n_seq * seq_pad, -1)
        ab_p = jnp.pad(ab.reshape(n_seq, seq, -1), ((0, 0), (0, seq_pad - seq), (0, 0))).reshape(n_seq * seq_pad, -1)
    s0_arr, s0_offset = s0
    o_a, s_new = _gdn(xz, ab_p, buf8, s0_arr, s0_offset, w["w_conv"], w["gdn_a_log"], w["gdn_dt_bias"],
                      w["g_gdn_out"], n_seq=n_seq, seq_rows=seq_pad, valid_len=seq)
    if seq_pad != seq:
        o_a = o_a.reshape(n_seq, seq_pad, -1)[:, :seq].reshape(m, -1)

    o_b = attend(proj, k_rows, v_rows)
    mk_rows, mv_rows, mem_offset, n_mem = mem
    o_c = _cross_attn(proj, mk_rows, mv_rows, mem_offset, n_seq=n_seq, seq=seq, n_mem=n_mem,
                      tq=_row_tile(seq, 512))

    x = _merge(x, o_a, o_b, o_c, proj, w["w_branch"], w["w_out"], w["g_post_mix"], tm=_row_tile(m, 512))
    x = _mlp(x, w["g_pre_mlp"], w["w_ff1"], w["w_ff2"], w["g_post_mlp"], tm=_row_tile(m, 1024), tf=1024)
    return x, proj, k_rows, v_rows, s_new


def kernel(x_prompt, x_sample, mem_prompt, cache_diff_k, cache_diff_v, page_table, state_gdn, cache_gdn_conv,
           cache_mem_k, cache_mem_v, g_pre_mix, w_in, w_conv, gdn_a_log, gdn_dt_bias, g_gdn_out, diff_lambda,
           g_diff_sub, g_mem, w_mem_k, w_mem_v, w_branch, w_out, g_post_mix, g_pre_mlp, w_ff1, w_ff2, g_post_mlp):
    bp, sp, d = x_prompt.shape
    bs, ts, _ = x_sample.shape
    depth = w_in.shape[0]
    n_mem = mem_prompt.shape[1]
    qkv_w = 3 * GDN_HEADS * GDN_DK
    assert sp >= GDN_CONV - 1 and ts >= GDN_CONV - 1
    dk_w = DIFF_HEADS * 2 * DIFF_DQK

    xp = x_prompt.reshape(bp * sp, d)
    xs = x_sample.reshape(bs * ts, d)
    memf = mem_prompt.reshape(bp * n_mem, d)

    c_a = qkv_w + BRANCH_W
    c_dq = c_a + 2 * GDN_HEADS
    c_dk = c_dq + dk_w
    c_xq = c_dk + 2 * dk_w

    outs = {k: [] for k in ("kp", "vp", "sp", "cp", "mk", "mv", "ks", "vs", "ss", "cs")}
    zero_buf = jnp.zeros((bp, SUBLANES, qkv_w), F32)
    zero_state = jnp.zeros((bp, GDN_HEADS, GDN_DK, GDN_DK), F32)
    state_all = state_gdn.reshape(depth * bs, GDN_HEADS, GDN_DK, GDN_DK)
    mk_all = cache_mem_k.reshape(depth * bs * n_mem * XA_HEADS, XA_DH)
    mv_all = cache_mem_v.reshape(depth * bs * n_mem * XA_HEADS, XA_DH)
    for l in range(depth):
        lam_init = 0.8 - 0.6 * math.exp(-0.3 * l)
        w = {
            "g_pre_mix": g_pre_mix[l],
            "w_main": jnp.concatenate([w_in[l][:, :c_a], w_in[l][:, c_dq:c_dk], w_in[l][:, c_xq:],
                                       w_in[l][:, c_dk:c_xq]], axis=1).astype(BF16),
            "w_ab": w_in[l][:, c_a:c_dq].astype(BF16),
            "w_conv": w_conv[l], "gdn_a_log": gdn_a_log[l], "gdn_dt_bias": gdn_dt_bias[l],
            "g_gdn_out": g_gdn_out[l],
            "w_branch": w_branch[l].astype(BF16), "w_out": w_out[l].astype(BF16), "g_post_mix": g_post_mix[l],
            "g_pre_mlp": g_pre_mlp[l], "w_ff1": w_ff1[l].astype(BF16), "w_ff2": w_ff2[l].astype(BF16),
            "g_post_mlp": g_post_mlp[l],
        }
        lam_p = diff_lambda[l]
        g_sub = g_diff_sub[l]

        w_mem = jnp.concatenate([w_mem_k[l], w_mem_v[l]], axis=1).astype(BF16)
        mk_p, mv_p = _mem_proj(memf, g_mem[l], w_mem, tm=_row_tile(bp * n_mem, 1024))
        attend_p = functools.partial(_diff_prompt, lam_p=lam_p, g_sub=g_sub, n_seq=bp, seq=sp,
                                     tq=_row_tile(sp, 256), lam_init=lam_init)
        xp, proj_p, k_p, v_p, s_p = _trunk_layer(xp, w, lam_init, n_seq=bp, seq=sp, buf8=zero_buf,
                                                 s0=(zero_state, 0), mem=(mk_p, mv_p, 0, n_mem), attend=attend_p)
        proj3 = proj_p.reshape(bp, sp, -1)
        outs["kp"].append(k_p.reshape(bp, sp, DIFF_HEADS, -1))
        outs["vp"].append(v_p.reshape(bp, sp, DIFF_HEADS, -1))
        outs["sp"].append(s_p)
        outs["cp"].append(proj3[:, sp - (GDN_CONV - 1):, :qkv_w])
        outs["mk"].append(mk_p.reshape(bp, n_mem, XA_HEADS, XA_DH))
        outs["mv"].append(mv_p.reshape(bp, n_mem, XA_HEADS, XA_DH))

        buf8 = jnp.pad(cache_gdn_conv[l], ((0, 0), (SUBLANES - (GDN_CONV - 1), 0), (0, 0)))
        attend_s = functools.partial(_diff_paged, lam_p=lam_p, g_sub=g_sub, cache_k=cache_diff_k,
                                     cache_v=cache_diff_v, page_table=page_table, layer=l, t_new=ts,
                                     lam_init=lam_init)
        xs, proj_s, k_s, v_s, s_s = _trunk_layer(xs, w, lam_init, n_seq=bs, seq=ts, buf8=buf8,
                                                 s0=(state_all, l * bs), mem=(mk_all, mv_all, l * bs, n_mem),
                                                 attend=attend_s)
        proj3 = proj_s.reshape(bs, ts, -1)
        outs["ks"].append(k_s.reshape(bs, ts, DIFF_HEADS, -1))
        outs["vs"].append(v_s.reshape(bs, ts, DIFF_HEADS, -1))
        outs["ss"].append(s_s)
        outs["cs"].append(proj3[:, ts - (GDN_CONV - 1):, :qkv_w])

    st = lambda k: jnp.stack(outs[k])
    return (xp.reshape(bp, sp, d), xs.reshape(bs, ts, d),
            st("kp"), st("vp"), st("sp"), st("cp"), st("mk"), st("mv"),
            st("ks"), st("vs"), st("ss"), st("cs"))
```
